```python
import math, functools
import jax, jax.numpy as jnp
from jax import lax
import numpy as np

D_MODEL = 1024
BATCH = 4
SEQ = 4096
DEPTH = 2

A_HEADS = 4
A_QK_DIM = 32
A_V_DIM = 2 * A_QK_DIM
B_HEADS = 4
B_HEAD_DIM = 64
C_HEADS = 4
C_HEAD_DIM = 128
CONV_K = 4
CHUNK = 64
Q_BLOCK = 128
D_FF = 2816
A_WIDTH = A_HEADS * A_V_DIM
B_WIDTH = B_HEADS * B_HEAD_DIM
C_WIDTH = C_HEADS * C_HEAD_DIM
MIX_WIDTH = A_WIDTH + B_WIDTH + C_WIDTH
SPLIT_SIZES = (A_HEADS * 2 * A_QK_DIM, A_HEADS * 2 * A_QK_DIM, A_WIDTH, B_WIDTH, B_WIDTH, B_WIDTH, 3 * C_WIDTH, C_WIDTH, C_HEADS, C_HEADS)
N_IN = sum(SPLIT_SIZES)
DEEPNORM_ALPHA = (2.0 * DEPTH) ** 0.25
DEEPNORM_BETA = (8.0 * DEPTH) ** -0.25
LN_EPS = 1e-5
RMS_EPS = 1e-6

kernel_name = 'hybrid_diff_stick_deltanet_macaron'


def layer_norm(x, g, b):
    xf = x.astype(jnp.float32)
    mu = jnp.mean(xf, axis=-1, keepdims=True)
    var = jnp.mean(jnp.square(xf - mu), axis=-1, keepdims=True)
    return ((xf - mu) * lax.rsqrt(var + LN_EPS) * g.astype(jnp.float32) + b.astype(jnp.float32)).astype(x.dtype)


def rms_norm(x, g):
    xf = x.astype(jnp.float32)
    y = xf * lax.rsqrt(jnp.mean(jnp.square(xf), axis=-1, keepdims=True) + RMS_EPS)
    return (y * g.astype(jnp.float32)).astype(x.dtype)


def l2_normalize(x):
    xf = x.astype(jnp.float32)
    return xf * lax.rsqrt(jnp.sum(jnp.square(xf), axis=-1, keepdims=True) + RMS_EPS)


def swiglu(x, w_gu, w_down):
    gate, up = jnp.split(x @ w_gu, 2, axis=-1)
    return (jax.nn.silu(gate) * up) @ w_down


def sweep_query_blocks(block_fn, q):
    b, h, s, d = q.shape
    nb = s // Q_BLOCK
    q_blocks = jnp.moveaxis(q.reshape(b, h, nb, Q_BLOCK, d), 2, 0)
    starts = jnp.arange(nb, dtype=jnp.int32) * Q_BLOCK
    out = lax.map(lambda qs: block_fn(qs[0], qs[1]), (q_blocks, starts))
    return jnp.moveaxis(out, 0, 2).reshape(b, h, s, out.shape[-1])


def diff_attention_block(q_blk, start, k1, k2, v, lam):
    q1, q2 = jnp.split(q_blk, 2, axis=-1)
    t = start + jnp.arange(Q_BLOCK)
    s = jnp.arange(k1.shape[2])
    causal = s[None, :] <= t[:, None]
    scale = A_QK_DIM ** -0.5

    def probs(qi, ki):
        sc = jnp.einsum('bhqd,bhkd->bhqk', qi, ki).astype(jnp.float32) * scale
        return jax.nn.softmax(jnp.where(causal, sc, -jnp.inf), axis=-1)

    w = probs(q1, k1) - lam * probs(q2, k2)
    return jnp.einsum('bhqk,bhkd->bhqd', w.astype(v.dtype), v)


def stick_breaking_block(q_blk, start, k, v):
    t = start + jnp.arange(Q_BLOCK)
    s = jnp.arange(k.shape[2])
    strict = s[None, :] < t[:, None]
    z = jnp.einsum('bhqd,bhkd->bhqk', q_blk, k).astype(jnp.float32) * (B_HEAD_DIM ** -0.5)
    log_stay = jnp.where(strict, jax.nn.log_sigmoid(-z), 0.0)
    log_tail = lax.cumsum(log_stay, axis=3, reverse=True) - log_stay
    w = jnp.where(strict, jnp.exp(jax.nn.log_sigmoid(z) + log_tail), 0.0)
    return jnp.einsum('bhqk,bhkd->bhqd', w.astype(v.dtype), v)


def causal_depthwise_conv(x, w):
    return lax.conv_general_dilated(x, w[:, None, :].astype(x.dtype), window_strides=(1,), padding=[(w.shape[0] - 1, 0)], dimension_numbers=('NWC', 'WIO', 'NWC'), feature_group_count=x.shape[-1])


def gated_delta_rule(q, k, v, g, beta):
    b, h, s, dk = q.shape
    dv = v.shape[-1]
    n = s // CHUNK
    q = q.astype(jnp.float32) * (dk ** -0.5)
    k = k.astype(jnp.float32)
    v = v.astype(jnp.float32)
    chunks = lambda a: a.reshape(b, h, n, CHUNK, *a.shape[3:])
    q, k, v, g, beta = chunks(q), chunks(k), chunks(v), chunks(g), chunks(beta)
    g = jnp.cumsum(g, axis=-1)
    lower = jnp.tril(jnp.ones((CHUNK, CHUNK), dtype=bool))
    strict = jnp.tril(jnp.ones((CHUNK, CHUNK), dtype=bool), -1)
    gdiff = g[..., :, None] - g[..., None, :]
    decay = jnp.where(lower, jnp.exp(jnp.where(lower, gdiff, 0.0)), 0.0)
    k_beta = k * beta[..., None]
    l_mat = jnp.where(strict, jnp.einsum('bhncd,bhned->bhnce', k_beta, k) * decay, 0.0)
    t_mat = l_mat + jnp.eye(CHUNK, dtype=jnp.float32)
    rhs = jnp.concatenate([v * beta[..., None], k_beta * jnp.exp(g)[..., None]], axis=-1)
    sol = lax.linalg.triangular_solve(t_mat, rhs, left_side=True, lower=True, unit_diagonal=True)
    u, w = sol[..., :dv], sol[..., dv:]
    intra = jnp.where(lower, jnp.einsum('bhncd,bhned->bhnce', q, k) * decay, 0.0)

    def step(state, inp):
        q_i, k_i, u_i, w_i, g_i, a_i = inp
        v_new = u_i - jnp.einsum('bhck,bhkv->bhcv', w_i, state)
        out = jnp.einsum('bhck,bhkv->bhcv', q_i * jnp.exp(g_i)[..., None], state) + jnp.einsum('bhcs,bhsv->bhcv', a_i, v_new)
        g_last = g_i[..., -1:]
        state = state * jnp.exp(g_last)[..., None] + jnp.einsum('bhck,bhcv->bhkv', k_i * jnp.exp(g_last - g_i)[..., None], v_new)
        return state, out

    xs = tuple(jnp.moveaxis(a, 2, 0) for a in (q, k, u, w, g, intra))
    state0 = jnp.zeros((b, h, dk, dv), jnp.float32)
    _, out = lax.scan(step, state0, xs)
    return jnp.moveaxis(out, 0, 2).reshape(b, h, s, dv)


def hybrid_mixer(xn, w_in, conv_w, dn_a_log, dn_dt_bias, dn_norm_g, diff_lambda, diff_norm_g, sb_norm_g, w_out, lambda_init):
    b, s, _ = xn.shape
    points = np.cumsum(SPLIT_SIZES)[:-1].tolist()
    qa, ka, va, qb, kb, vb, qkv_c, z_c, beta_c, a_c = jnp.split(xn @ w_in, points, axis=-1)
    heads = lambda t, nh: t.reshape(b, s, nh, -1).transpose(0, 2, 1, 3)
    tokens = lambda t: t.transpose(0, 2, 1, 3).reshape(b, s, -1)

    k1, k2 = jnp.split(heads(ka, A_HEADS), 2, axis=-1)
    lf = diff_lambda.astype(jnp.float32)
    lam = jnp.exp(jnp.sum(lf[0] * lf[1])) - jnp.exp(jnp.sum(lf[2] * lf[3])) + lambda_init
    oa = sweep_query_blocks(functools.partial(diff_attention_block, k1=k1, k2=k2, v=heads(va, A_HEADS), lam=lam), heads(qa, A_HEADS))
    oa = rms_norm(oa, diff_norm_g) * (1.0 - lambda_init)

    ob = sweep_query_blocks(functools.partial(stick_breaking_block, k=heads(kb, B_HEADS), v=heads(vb, B_HEADS)), heads(qb, B_HEADS))
    ob = rms_norm(ob, sb_norm_g)

    qc, kc, vc = jnp.split(jax.nn.silu(causal_depthwise_conv(qkv_c, conv_w)), 3, axis=-1)
    g = -jnp.exp(dn_a_log.astype(jnp.float32)) * jax.nn.softplus(a_c.astype(jnp.float32) + dn_dt_bias.astype(jnp.float32))
    beta = jax.nn.sigmoid(beta_c.astype(jnp.float32))
    oc = gated_delta_rule(l2_normalize(heads(qc, C_HEADS)), l2_normalize(heads(kc, C_HEADS)), heads(vc, C_HEADS), g.transpose(0, 2, 1), beta.transpose(0, 2, 1))
    oc = rms_norm(oc, dn_norm_g) * jax.nn.silu(heads(z_c, C_HEADS).astype(jnp.float32))

    o = jnp.concatenate([tokens(oa).astype(xn.dtype), tokens(ob).astype(xn.dtype), tokens(oc).astype(xn.dtype)], axis=-1)
    return o @ w_out


def setup_inputs(seed: int = 0) -> dict:
    key = jax.random.key(seed)
    ks = jax.random.split(key, 16)
    f32 = jnp.float32
    x = jax.random.normal(ks[0], (BATCH, SEQ, D_MODEL), f32)
    ffn1_w_gu = jax.random.normal(ks[1], (DEPTH, D_MODEL, 2 * D_FF), f32) * D_MODEL ** -0.5
    ffn1_w_down = jax.random.normal(ks[2], (DEPTH, D_FF, D_MODEL), f32) * (D_FF ** -0.5 * DEEPNORM_BETA)
    ffn2_w_gu = jax.random.normal(ks[3], (DEPTH, D_MODEL, 2 * D_FF), f32) * D_MODEL ** -0.5
    ffn2_w_down = jax.random.normal(ks[4], (DEPTH, D_FF, D_MODEL), f32) * (D_FF ** -0.5 * DEEPNORM_BETA)
    ln_g = 1.0 + 0.02 * jax.random.normal(ks[5], (DEPTH, 3, D_MODEL), f32)
    ln_b = 0.02 * jax.random.normal(ks[6], (DEPTH, 3, D_MODEL), f32)
    w_in = jax.random.normal(ks[7], (DEPTH, D_MODEL, N_IN), f32) * D_MODEL ** -0.5
    conv_w = jax.random.normal(ks[8], (DEPTH, CONV_K, 3 * C_WIDTH), f32) * CONV_K ** -0.5
    dn_a_log = jnp.log(jax.random.uniform(ks[9], (DEPTH, C_HEADS), f32, 1.0, 16.0))
    dt = jnp.exp(jax.random.uniform(ks[10], (DEPTH, C_HEADS), f32, math.log(1e-3), math.log(1e-1)))
    dn_dt_bias = dt + jnp.log(-jnp.expm1(-dt))
    dn_norm_g = 1.0 + 0.02 * jax.random.normal(ks[11], (DEPTH, C_HEAD_DIM), f32)
    diff_lambda = 0.1 * jax.random.normal(ks[12], (DEPTH, 4, A_QK_DIM), f32)
    diff_norm_g = 1.0 + 0.02 * jax.random.normal(ks[13], (DEPTH, A_V_DIM), f32)
    sb_norm_g = 1.0 + 0.02 * jax.random.normal(ks[14], (DEPTH, B_HEAD_DIM), f32)
    w_out = jax.random.normal(ks[15], (DEPTH, MIX_WIDTH, D_MODEL), f32) * (MIX_WIDTH ** -0.5 * DEEPNORM_BETA)
    return {'x': x, 'ffn1_w_gu': ffn1_w_gu, 'ffn1_w_down': ffn1_w_down, 'ffn2_w_gu': ffn2_w_gu, 'ffn2_w_down': ffn2_w_down, 'ln_g': ln_g, 'ln_b': ln_b, 'w_in': w_in, 'conv_w': conv_w, 'dn_a_log': dn_a_log, 'dn_dt_bias': dn_dt_bias, 'dn_norm_g': dn_norm_g, 'diff_lambda': diff_lambda, 'diff_norm_g': diff_norm_g, 'sb_norm_g': sb_norm_g, 'w_out': w_out}


def reference(x, ffn1_w_gu, ffn1_w_down, ffn2_w_gu, ffn2_w_down, ln_g, ln_b, w_in, conv_w, dn_a_log, dn_dt_bias, dn_norm_g, diff_lambda, diff_norm_g, sb_norm_g, w_out):
    for l in range(DEPTH):
        lambda_init = 0.8 - 0.6 * math.exp(-0.3 * l)
        x = layer_norm(DEEPNORM_ALPHA * x + 0.5 * swiglu(x, ffn1_w_gu[l], ffn1_w_down[l]), ln_g[l, 0], ln_b[l, 0])
        mix = hybrid_mixer(x, w_in[l], conv_w[l], dn_a_log[l], dn_dt_bias[l], dn_norm_g[l], diff_lambda[l], diff_norm_g[l], sb_norm_g[l], w_out[l], lambda_init)
        x = layer_norm(DEEPNORM_ALPHA * x + mix, ln_g[l, 1], ln_b[l, 1])
        x = layer_norm(DEEPNORM_ALPHA * x + 0.5 * swiglu(x, ffn2_w_gu[l], ffn2_w_down[l]), ln_g[l, 2], ln_b[l, 2])
    return x
```

```python
import functools
import math

import jax
import jax.numpy as jnp
from jax import lax
from jax.experimental import pallas as pl
from jax.experimental.pallas import tpu as pltpu

F32 = jnp.float32
BF16 = jnp.bfloat16

DEPTH = 2
A_HEADS, A_QK_DIM, A_V_DIM = 4, 32, 64
B_HEADS, B_HEAD_DIM = 4, 64
C_HEADS, C_HEAD_DIM = 4, 128
CONV_K = 4
D_FF = 2816
A_WIDTH, B_WIDTH, C_WIDTH = 256, 256, 512
DEEPNORM_ALPHA = (2.0 * DEPTH) ** 0.25
LN_EPS = 1e-5
RMS_EPS = 1e-6
NEG_BIG = -1e30

V7X_LANES = 128
V7X_SUBLANES = 8
V7X_VMEM_LIMIT_BYTES = 56 * 1024 * 1024

FFN_TM = 1024
FFN_TF = 256
PROJ_TM = 256
ATT_T = 256
GDN_C = 128
OUT_TM = 512


def _cparams(sem):
    return pltpu.CompilerParams(dimension_semantics=sem, vmem_limit_bytes=V7X_VMEM_LIMIT_BYTES)


def _layer_norm(y, g, b):
    mu = jnp.mean(y, axis=-1, keepdims=True)
    d = y - mu
    var = jnp.mean(d * d, axis=-1, keepdims=True)
    return d * lax.rsqrt(var + LN_EPS) * g + b


def _silu(x):
    return x / (1.0 + jnp.exp(-x))


def _softplus(x):
    return jnp.maximum(x, 0.0) + jnp.log1p(jnp.exp(-jnp.abs(x)))


def _dot(a, b):
    return jnp.dot(a, b, preferred_element_type=F32)


def _dot_nt(a, b):
    return lax.dot_general(a, b, (((1,), (1,)), ((), ())), preferred_element_type=F32)


def _split3(x):
    hi = x.astype(BF16)
    r = x - hi.astype(F32)
    mid = r.astype(BF16)
    lo = (r - mid.astype(F32)).astype(BF16)
    return hi, mid, lo


def _dot_mask_f32(mask_bf16, x):
    hi, mid, lo = _split3(x)
    return _dot(mask_bf16, hi) + _dot(mask_bf16, mid) + _dot(mask_bf16, lo)


def _ffn_kernel(x_ref, wg_ref, wu_ref, wd_ref, g_ref, b_ref, o_ref, xb_ref, acc_ref):
    j = pl.program_id(1)

    @pl.when(j == 0)
    def _():
        xb_ref[...] = x_ref[...].astype(BF16)
        acc_ref[...] = jnp.zeros_like(acc_ref)

    xb = xb_ref[...]
    gate = _dot(xb, wg_ref[...])
    up = _dot(xb, wu_ref[...])
    act = (_silu(gate) * up).astype(BF16)
    acc_ref[...] += _dot(act, wd_ref[...])

    @pl.when(j == pl.num_programs(1) - 1)
    def _():
        y = DEEPNORM_ALPHA * x_ref[...] + 0.5 * acc_ref[...]
        o_ref[...] = _layer_norm(y, g_ref[...], b_ref[...])


def ffn_ln(x, w_gu, w_down, g, b, tm=FFN_TM, tf=FFN_TF):
    t, d = x.shape
    ff = w_down.shape[0]
    tm = min(tm, t)
    nf = ff // tf
    return pl.pallas_call(
        _ffn_kernel,
        grid=(t // tm, nf),
        in_specs=[
            pl.BlockSpec((tm, d), lambda i, j: (i, 0)),
            pl.BlockSpec((d, tf), lambda i, j: (0, j)),
            pl.BlockSpec((d, tf), lambda i, j: (0, j + nf)),
            pl.BlockSpec((tf, d), lambda i, j: (j, 0)),
            pl.BlockSpec((1, d), lambda i, j: (0, 0)),
            pl.BlockSpec((1, d), lambda i, j: (0, 0)),
        ],
        out_specs=pl.BlockSpec((tm, d), lambda i, j: (i, 0)),
        out_shape=jax.ShapeDtypeStruct((t, d), F32),
        scratch_shapes=[pltpu.VMEM((tm, d), BF16), pltpu.VMEM((tm, d), F32)],
        compiler_params=_cparams(("parallel", "arbitrary")),
        name="ffn_ln",
    )(x, w_gu, w_gu, w_down, g, b)


def _proj_kernel(x_ref, wk_ref, wt_ref, wc_ref, wg_ref,
                 ka_ref, kb_ref, qta_ref, vta_ref, qtb_ref, vtb_ref, c_ref, gate_ref):
    xb = x_ref[...].astype(BF16)
    k = _dot(xb, wk_ref[...])
    ka_ref[...] = k[:, :A_WIDTH].astype(BF16)
    kb_ref[...] = k[:, A_WIDTH:].astype(BF16)
    t = _dot_nt(wt_ref[...], xb)
    qta_ref[0] = (t[0:256] * (A_QK_DIM ** -0.5)).astype(BF16)
    vta_ref[0] = t[256:512].astype(BF16)
    qtb_ref[0] = (t[512:768] * (B_HEAD_DIM ** -0.5)).astype(BF16)
    vtb_ref[0] = t[768:1024].astype(BF16)
    c_ref[...] = _dot(xb, wc_ref[...])
    gate_ref[...] = _dot(xb, wg_ref[...])


def in_proj(x, wk, wt, wc, wg, tm=PROJ_TM):
    t, d = x.shape
    nt = t // tm
    full = lambda a: pl.BlockSpec(a.shape, lambda i: (0,) * a.ndim)
    row = lambda w: pl.BlockSpec((tm, w), lambda i: (i, 0))
    tr = pl.BlockSpec((1, 256, tm), lambda i: (i, 0, 0))
    tr_shape = jax.ShapeDtypeStruct((nt, 256, tm), BF16)
    return pl.pallas_call(
        _proj_kernel,
        grid=(nt,),
        in_specs=[row(d), full(wk), full(wt), full(wc), full(wg)],
        out_specs=[row(256), row(256), tr, tr, tr, tr, row(wc.shape[1]), row(V7X_LANES)],
        out_shape=[
            jax.ShapeDtypeStruct((t, 256), BF16), jax.ShapeDtypeStruct((t, 256), BF16),
            tr_shape, tr_shape, tr_shape, tr_shape,
            jax.ShapeDtypeStruct((t, wc.shape[1]), F32), jax.ShapeDtypeStruct((t, V7X_LANES), F32),
        ],
        compiler_params=_cparams(("parallel",)),
        name="in_proj",
    )(x, wk, wt, wc, wg)


def _diff_attn_kernel(q_ref, k_ref, v_ref, lam_ref, g_ref, o_ref,
                      qm_ref, m_ref, l_ref, acc_ref, ot_ref, *, lambda_init, tile):
    qi = pl.program_id(1)
    qt = q_ref[0]
    row = lax.broadcasted_iota(jnp.int32, qt.shape, 0)
    for c in range(2 * A_HEADS):
        keep = (row >= c * A_QK_DIM) & (row < (c + 1) * A_QK_DIM)
        qm_ref[c] = jnp.where(keep, qt, jnp.zeros_like(qt))
    m_ref[...] = jnp.full(m_ref.shape, NEG_BIG, F32)
    l_ref[...] = jnp.zeros_like(l_ref)
    acc_ref[...] = jnp.zeros_like(acc_ref)

    def do_tile(j, masked):
        kb = k_ref[pl.ds(pl.multiple_of(j * tile, tile), tile), :]
        vt = v_ref[j]
        if masked:
            key = lax.broadcasted_iota(jnp.int32, (tile, tile), 0)
            qry = lax.broadcasted_iota(jnp.int32, (tile, tile), 1)
            causal = key <= qry
        for c in range(2 * A_HEADS):
            h = c // 2
            s = _dot(kb, qm_ref[c])
            if masked:
                s = jnp.where(causal, s, NEG_BIG)
            m_old = m_ref[c]
            m_new = jnp.maximum(m_old, jnp.max(s, axis=0, keepdims=True))
            p = jnp.exp(s - m_new)
            a = jnp.exp(m_old - m_new)
            l_ref[c] = a * l_ref[c] + jnp.sum(p, axis=0, keepdims=True)
            pv = _dot(vt[h * A_V_DIM:(h + 1) * A_V_DIM, :], p.astype(BF16))
            acc_ref[c] = a * acc_ref[c] + pv
            m_ref[c] = m_new

    def body(j, carry):
        do_tile(j, False)
        return carry

    lax.fori_loop(0, qi, body, 0)
    do_tile(qi, True)

    lf = lam_ref[...]
    lam = (jnp.exp(jnp.sum(lf[0:1] * lf[1:2], axis=-1, keepdims=True))
           - jnp.exp(jnp.sum(lf[2:3] * lf[3:4], axis=-1, keepdims=True)) + lambda_init)
    g = g_ref[...]
    for h in range(A_HEADS):
        o = acc_ref[2 * h] / l_ref[2 * h] - lam * (acc_ref[2 * h + 1] / l_ref[2 * h + 1])
        ms = jnp.mean(o * o, axis=0, keepdims=True)
        ot_ref[h * A_V_DIM:(h + 1) * A_V_DIM, :] = o * lax.rsqrt(ms + RMS_EPS) * g * (1.0 - lambda_init)
    o_ref[...] = ot_ref[...].T.astype(BF16)


def diff_attn(qt, k, vt, diff_lambda, norm_g, batch, lambda_init, tile=ATT_T):
    t = k.shape[0]
    s = t // batch
    nq = s // tile
    kern = functools.partial(_diff_attn_kernel, lambda_init=lambda_init, tile=tile)
    return pl.pallas_call(
        kern,
        grid=(batch, nq),
        in_specs=[
            pl.BlockSpec((1, 256, tile), lambda b, i: (b * nq + i, 0, 0)),
            pl.BlockSpec((s, 256), lambda b, i: (b, 0)),
            pl.BlockSpec((nq, 256, tile), lambda b, i: (b, 0, 0)),
            pl.BlockSpec(diff_lambda.shape, lambda b, i: (0, 0)),
            pl.BlockSpec((A_V_DIM, 1), lambda b, i: (0, 0)),
        ],
        out_specs=pl.BlockSpec((tile, 256), lambda b, i: (b * nq + i, 0)),
        out_shape=jax.ShapeDtypeStruct((t, 256), BF16),
        scratch_shapes=[
            pltpu.VMEM((2 * A_HEADS, 256, tile), BF16),
            pltpu.VMEM((2 * A_HEADS, 1, tile), F32),
            pltpu.VMEM((2 * A_HEADS, 1, tile), F32),
            pltpu.VMEM((2 * A_HEADS, A_V_DIM, tile), F32),
            pltpu.VMEM((256, tile), F32),
        ],
        compiler_params=_cparams(("parallel", "arbitrary")),
        name="diff_attn",
    )(qt, k, vt, diff_lambda, norm_g.reshape(A_V_DIM, 1))


def _stick_attn_kernel(q_ref, k_ref, v_ref, g_ref, o_ref, qm_ref, carry_ref, acc_ref, ot_ref, *, tile):
    qi = pl.program_id(1)
    qt = q_ref[0]
    row = lax.broadcasted_iota(jnp.int32, qt.shape, 0)
    for h in range(B_HEADS):
        keep = (row >= h * B_HEAD_DIM) & (row < (h + 1) * B_HEAD_DIM)
        qm_ref[h] = jnp.where(keep, qt, jnp.zeros_like(qt))
    carry_ref[...] = jnp.zeros_like(carry_ref)
    acc_ref[...] = jnp.zeros_like(acc_ref)

    key = lax.broadcasted_iota(jnp.int32, (tile, tile), 0)
    qry = lax.broadcasted_iota(jnp.int32, (tile, tile), 1)
    upper = (qry > key).astype(BF16)
    strict = key < qry

    def do_tile(j, masked):
        kb = k_ref[pl.ds(pl.multiple_of(j * tile, tile), tile), :]
        vt = v_ref[j]
        for h in range(B_HEADS):
            z = _dot(kb, qm_ref[h])
            sp = _softplus(z)
            if masked:
                sp = jnp.where(strict, sp, 0.0)
            tail = _dot_mask_f32(upper, sp)
            e = z - sp - tail - carry_ref[h]
            w = jnp.exp(e)
            if masked:
                w = jnp.where(strict, w, 0.0)
            acc_ref[h] += _dot(vt[h * B_HEAD_DIM:(h + 1) * B_HEAD_DIM, :], w.astype(BF16))
            carry_ref[h] += jnp.sum(sp, axis=0, keepdims=True)

    do_tile(qi, True)

    def body(jj, carry):
        do_tile(qi - 1 - jj, False)
        return carry

    lax.fori_loop(0, qi, body, 0)

    g = g_ref[...]
    for h in range(B_HEADS):
        o = acc_ref[h]
        ms = jnp.mean(o * o, axis=0, keepdims=True)
        ot_ref[h * B_HEAD_DIM:(h + 1) * B_HEAD_DIM, :] = o * lax.rsqrt(ms + RMS_EPS) * g
    o_ref[...] = ot_ref[...].T.astype(BF16)


def stick_attn(qt, k, vt, norm_g, batch, tile=ATT_T):
    t = k.shape[0]
    s = t // batch
    nq = s // tile
    kern = functools.partial(_stick_attn_kernel, tile=tile)
    return pl.pallas_call(
        kern,
        grid=(batch, nq),
        in_specs=[
            pl.BlockSpec((1, 256, tile), lambda b, i: (b * nq + i, 0, 0)),
            pl.BlockSpec((s, 256), lambda b, i: (b, 0)),
            pl.BlockSpec((nq, 256, tile), lambda b, i: (b, 0, 0)),
            pl.BlockSpec((B_HEAD_DIM, 1), lambda b, i: (0, 0)),
        ],
        out_specs=pl.BlockSpec((tile, 256), lambda b, i: (b * nq + i, 0)),
        out_shape=jax.ShapeDtypeStruct((t, 256), BF16),
        scratch_shapes=[
            pltpu.VMEM((B_HEADS, 256, tile), BF16),
            pltpu.VMEM((B_HEADS, 1, tile), F32),
            pltpu.VMEM((B_HEADS, B_HEAD_DIM, tile), F32),
            pltpu.VMEM((256, tile), F32),
        ],
        compiler_params=_cparams(("parallel", "arbitrary")),
        name="stick_attn",
    )(qt, k, vt, norm_g.reshape(B_HEAD_DIM, 1))


def _gdn_prep_kernel(x_ref, halo_ref, gate_ref, cw_ref, alog_ref, dtb_ref,
                     u_ref, w_ref, qg_ref, intra_ref, kdt_ref, eg_ref, xs_ref, *, chunk):
    ci = pl.program_id(1)
    c = chunk
    halo = jnp.where(ci > 0, halo_ref[...], 0.0)
    xs_ref[0:V7X_SUBLANES, :] = halo
    xs_ref[V7X_SUBLANES:, :] = x_ref[...]
    cw = cw_ref[...]
    conv = cw[0:1] * xs_ref[pl.ds(V7X_SUBLANES - 3, c), :]
    for i in range(1, CONV_K):
        conv = conv + cw[i:i + 1] * xs_ref[pl.ds(V7X_SUBLANES - 3 + i, c), :]
    y = _silu(conv)

    gates = gate_ref[...]
    beta_all = 1.0 / (1.0 + jnp.exp(-gates))
    g_all = -jnp.exp(alog_ref[...]) * _softplus(gates + dtb_ref[...])

    r = lax.broadcasted_iota(jnp.int32, (c, c), 0)
    q_ = lax.broadcasted_iota(jnp.int32, (c, c), 1)
    lo_incl = q_ <= r
    lo_strict = q_ < r
    lo_incl_b = lo_incl.astype(BF16)
    lo_strict_f = lo_strict.astype(F32)
    scale = C_HEAD_DIM ** -0.5

    for h in range(C_HEADS):
        sl = slice(h * C_HEAD_DIM, (h + 1) * C_HEAD_DIM)
        q = y[:, h * C_HEAD_DIM:(h + 1) * C_HEAD_DIM]
        k = y[:, C_WIDTH + h * C_HEAD_DIM:C_WIDTH + (h + 1) * C_HEAD_DIM]
        v = y[:, 2 * C_WIDTH + h * C_HEAD_DIM:2 * C_WIDTH + (h + 1) * C_HEAD_DIM]
        q = q * lax.rsqrt(jnp.sum(q * q, axis=-1, keepdims=True) + RMS_EPS)
        k = k * lax.rsqrt(jnp.sum(k * k, axis=-1, keepdims=True) + RMS_EPS)
        beta_b = jnp.broadcast_to(beta_all[:, h:h + 1], (c, C_HEAD_DIM))
        g_b = jnp.broadcast_to(g_all[:, C_HEADS + h:C_HEADS + h + 1], (c, C_HEAD_DIM))

        gc_b = _dot_mask_f32(lo_incl_b, g_b)
        total = gc_b[c - 1:c, :]
        gdiff = _dot_mask_f32(lo_incl_b, g_b * lo_strict_f)
        decay = jnp.where(lo_incl, jnp.exp(gdiff), 0.0)

        kbeta = k * beta_b
        kf = k.astype(BF16)
        lmat = jnp.where(lo_strict, _dot_nt(kbeta.astype(BF16), kf) * decay, 0.0)
        intra = jnp.where(lo_incl, _dot_nt(q.astype(BF16), kf) * decay, 0.0) * scale

        ymat = -jnp.where((r >> 1) == (q_ >> 1), lmat, 0.0)
        bs = 2
        while bs < c:
            sh = bs.bit_length() - 1
            cb = jnp.where(((r >> (sh + 1)) == (q_ >> (sh + 1))) & ((r >> sh) != (q_ >> sh)), lmat, 0.0)
            yb = ymat.astype(BF16)
            p = cb + _dot(yb, cb.astype(BF16))
            ymat = ymat - p - _dot(p.astype(BF16), yb)
            bs *= 2

        rhs = jnp.concatenate([v * beta_b, kbeta * jnp.exp(gc_b)], axis=1)
        sol = rhs + _dot(ymat.astype(BF16), rhs.astype(BF16))
        u_ref[:, sl] = sol[:, :C_HEAD_DIM]
        w_ref[:, sl] = sol[:, C_HEAD_DIM:].astype(BF16)
        qg_ref[:, sl] = (q * jnp.exp(gc_b) * scale).astype(BF16)
        intra_ref[:, sl] = intra.astype(BF16)
        kd = k * jnp.exp(total - gc_b)
        kdt_ref[0, sl, :] = kd.T.astype(BF16)
        eg_ref[0, :, sl] = jnp.broadcast_to(jnp.exp(total), (V7X_SUBLANES, C_HEAD_DIM))


def gdn_prep(cbuf, gates, conv_w, alog_row, dtb_row, batch, chunk=GDN_C):
    t = cbuf.shape[0]
    s = t // batch
    nc = s // chunk
    hb = chunk // V7X_SUBLANES
    w3 = 3 * C_WIDTH
    kern = functools.partial(_gdn_prep_kernel, chunk=chunk)
    row = lambda w: pl.BlockSpec((chunk, w), lambda b, i: (b * nc + i, 0))
    return pl.pallas_call(
        kern,
        grid=(batch, nc),
        in_specs=[
            row(w3),
            pl.BlockSpec((V7X_SUBLANES, w3), lambda b, i: (jnp.maximum((b * nc + i) * hb - 1, 0), 0)),
            row(V7X_LANES),
            pl.BlockSpec(conv_w.shape, lambda b, i: (0, 0)),
            pl.BlockSpec((1, V7X_LANES), lambda b, i: (0, 0)),
            pl.BlockSpec((1, V7X_LANES), lambda b, i: (0, 0)),
        ],
        out_specs=[
            row(C_WIDTH), row(C_WIDTH), row(C_WIDTH), row(C_WIDTH),
            pl.BlockSpec((1, C_WIDTH, chunk), lambda b, i: (b * nc + i, 0, 0)),
            pl.BlockSpec((1, V7X_SUBLANES, C_WIDTH), lambda b, i: (b * nc + i, 0, 0)),
        ],
        out_shape=[
            jax.ShapeDtypeStruct((t, C_WIDTH), F32),
            jax.ShapeDtypeStruct((t, C_WIDTH), BF16),
            jax.ShapeDtypeStruct((t, C_WIDTH), BF16),
            jax.ShapeDtypeStruct((t, C_WIDTH), BF16),
            jax.ShapeDtypeStruct((t // chunk, C_WIDTH, chunk), BF16),
            jax.ShapeDtypeStruct((t // chunk, V7X_SUBLANES, C_WIDTH), F32),
        ],
        scratch_shapes=[pltpu.VMEM((chunk + V7X_SUBLANES, w3), F32)],
        compiler_params=_cparams(("parallel", "arbitrary")),
        name="gdn_prep",
    )(cbuf, cbuf, gates, conv_w, alog_row, dtb_row)


def _gdn_seq_kernel(u_ref, w_ref, qg_ref, intra_ref, kdt_ref, eg_ref, z_ref, gn_ref, o_ref, s_ref):
    @pl.when(pl.program_id(1) == 0)
    def _():
        s_ref[...] = jnp.zeros_like(s_ref)

    gn = gn_ref[...]
    for h in range(C_HEADS):
        sl = slice(h * C_HEAD_DIM, (h + 1) * C_HEAD_DIM)
        state = s_ref[h]
        sb = state.astype(BF16)
        vn = u_ref[:, sl] - _dot(w_ref[:, sl], sb)
        vnb = vn.astype(BF16)
        o = _dot(qg_ref[:, sl], sb) + _dot(intra_ref[:, sl], vnb)
        s_ref[h] = state * eg_ref[0, 0:1, sl] + _dot(kdt_ref[0, sl, :], vnb)
        on = o * lax.rsqrt(jnp.mean(o * o, axis=-1, keepdims=True) + RMS_EPS) * gn
        o_ref[:, sl] = (on * _silu(z_ref[:, sl])).astype(BF16)


def gdn_seq(u, w, qg, intra, kdt, eg, cbuf, norm_g, batch, chunk=GDN_C):
    t = u.shape[0]
    nc = t // batch // chunk
    row = pl.BlockSpec((chunk, C_WIDTH), lambda b, i: (b * nc + i, 0))
    return pl.pallas_call(
        _gdn_seq_kernel,
        grid=(batch, nc),
        in_specs=[
            row, row, row, row,
            pl.BlockSpec((1, C_WIDTH, chunk), lambda b, i: (b * nc + i, 0, 0)),
            pl.BlockSpec((1, V7X_SUBLANES, C_WIDTH), lambda b, i: (b * nc + i, 0, 0)),
            pl.BlockSpec((chunk, C_WIDTH), lambda b, i: (b * nc + i, 3)),
            pl.BlockSpec((1, C_HEAD_DIM), lambda b, i: (0, 0)),
        ],
        out_specs=row,
        out_shape=jax.ShapeDtypeStruct((t, C_WIDTH), BF16),
        scratch_shapes=[pltpu.VMEM((C_HEADS, C_HEAD_DIM, C_HEAD_DIM), F32)],
        compiler_params=_cparams(("parallel", "arbitrary")),
        name="gdn_seq",
    )(u, w, qg, intra, kdt, eg, cbuf, norm_g.reshape(1, C_HEAD_DIM))


def _out_kernel(x_ref, oa_ref, ob_ref, oc_ref, w_ref, g_ref, b_ref, o_ref):
    mix = _dot(oa_ref[...], w_ref[0:A_WIDTH, :])
    mix = mix + _dot(ob_ref[...], w_ref[A_WIDTH:A_WIDTH + B_WIDTH, :])
    mix = mix + _dot(oc_ref[...], w_ref[A_WIDTH + B_WIDTH:, :])
    y = DEEPNORM_ALPHA * x_ref[...] + mix
    o_ref[...] = _layer_norm(y, g_ref[...], b_ref[...])


def out_ln(x, oa, ob, oc, w_out, g, b, tm=OUT_TM):
    t, d = x.shape
    tm = min(tm, t)
    row = lambda w: pl.BlockSpec((tm, w), lambda i: (i, 0))
    return pl.pallas_call(
        _out_kernel,
        grid=(t // tm,),
        in_specs=[row(d), row(A_WIDTH), row(B_WIDTH), row(C_WIDTH),
                  pl.BlockSpec(w_out.shape, lambda i: (0, 0)),
                  pl.BlockSpec((1, d), lambda i: (0, 0)), pl.BlockSpec((1, d), lambda i: (0, 0))],
        out_specs=row(d),
        out_shape=jax.ShapeDtypeStruct((t, d), F32),
        compiler_params=_cparams(("parallel",)),
        name="out_ln",
    )(x, oa, ob, oc, w_out, g, b)


def _split_w_in(w):
    o = 0
    parts = {}
    for name, n in (("qa", 256), ("ka", 256), ("va", 256), ("qb", 256), ("kb", 256), ("vb", 256),
                    ("qkv", 3 * C_WIDTH), ("z", C_WIDTH), ("beta", C_HEADS), ("a", C_HEADS)):
        parts[name] = w[:, o:o + n]
        o += n
    wk = jnp.concatenate([parts["ka"], parts["kb"]], axis=1).astype(BF16)
    wt = jnp.concatenate([parts["qa"], parts["va"], parts["qb"], parts["vb"]], axis=1).T.astype(BF16)
    wc = jnp.concatenate([parts["qkv"], parts["z"]], axis=1).astype(BF16)
    pad = jnp.zeros((w.shape[0], V7X_LANES - 2 * C_HEADS), w.dtype)
    wg = jnp.concatenate([parts["beta"], parts["a"], pad], axis=1).astype(BF16)
    return wk, wt, wc, wg


def _gate_row(vals):
    row = jnp.zeros((1, V7X_LANES), F32)
    return row.at[0, C_HEADS:2 * C_HEADS].set(vals.astype(F32))


def kernel(x, ffn1_w_gu, ffn1_w_down, ffn2_w_gu, ffn2_w_down, ln_g, ln_b, w_in, conv_w, dn_a_log, dn_dt_bias,
           dn_norm_g, diff_lambda, diff_norm_g, sb_norm_g, w_out):
    batch, seq, d = x.shape
    h = x.reshape(batch * seq, d)
    for l in range(DEPTH):
        lambda_init = 0.8 - 0.6 * math.exp(-0.3 * l)
        ln = lambda i: (ln_g[l, i].reshape(1, d), ln_b[l, i].reshape(1, d))
        h = ffn_ln(h, ffn1_w_gu[l].astype(BF16), ffn1_w_down[l].astype(BF16), *ln(0))
        wk, wt, wc, wg = _split_w_in(w_in[l])
        ka, kb, qta, vta, qtb, vtb, cbuf, gates = in_proj(h, wk, wt, wc, wg)
        oa = diff_attn(qta, ka, vta, diff_lambda[l], diff_norm_g[l], batch, lambda_init)
        ob = stick_attn(qtb, kb, vtb, sb_norm_g[l], batch)
        u, w, qg, intra, kdt, eg = gdn_prep(cbuf, gates, conv_w[l], _gate_row(dn_a_log[l]),
                                            _gate_row(dn_dt_bias[l]), batch)
        oc = gdn_seq(u, w, qg, intra, kdt, eg, cbuf, dn_norm_g[l], batch)
        h = out_ln(h, oa, ob, oc, w_out[l].astype(BF16), *ln(1))
        h = ffn_ln(h, ffn2_w_gu[l].astype(BF16), ffn2_w_down[l].astype(BF16), *ln(2))
    return h.reshape(batch, seq, d)
```

```python
import functools
import math

import jax
import jax.numpy as jnp
from jax import lax
from jax.experimental import pallas as pl
from jax.experimental.pallas import tpu as pltpu

F32 = jnp.float32
BF16 = jnp.bfloat16

DEPTH = 2
A_HEADS, A_QK_DIM, A_V_DIM = 4, 32, 64
B_HEADS, B_HEAD_DIM = 4, 64
C_HEADS, C_HEAD_DIM = 4, 128
CONV_K = 4
D_FF = 2816
A_WIDTH, B_WIDTH, C_WIDTH = 256, 256, 512
DEEPNORM_ALPHA = (2.0 * DEPTH) ** 0.25
LN_EPS = 1e-5
RMS_EPS = 1e-6
NEG_BIG = -1e30
STICK_DEAD_LOG = 104.0

V7X_LANES = 128
V7X_SUBLANES = 8
V7X_VMEM_LIMIT_BYTES = 56 * 1024 * 1024

FFN_TM = 1024
FFN_TF = 256
PROJ_TM = 256
ATT_T = 256
GDN_C = 256
OUT_TM = 512


def _cparams(sem):
    return pltpu.CompilerParams(dimension_semantics=sem, vmem_limit_bytes=V7X_VMEM_LIMIT_BYTES)


def _layer_norm(y, g, b):
    mu = jnp.mean(y, axis=-1, keepdims=True)
    d = y - mu
    var = jnp.mean(d * d, axis=-1, keepdims=True)
    return d * lax.rsqrt(var + LN_EPS) * g + b


def _silu(x):
    return x / (1.0 + jnp.exp(-x))


def _softplus(x):
    return jnp.maximum(x, 0.0) + jnp.log1p(jnp.exp(-jnp.abs(x)))


def _dot(a, b):
    return jnp.dot(a, b, preferred_element_type=F32)


def _dot_nt(a, b):
    return lax.dot_general(a, b, (((1,), (1,)), ((), ())), preferred_element_type=F32)


def _split2_rows(x):
    hi = x.astype(BF16)
    lo = (x - hi.astype(F32)).astype(BF16)
    return jnp.concatenate([hi, lo], axis=0)


def _ffn_kernel(x_ref, wg_ref, wu_ref, wd_ref, g_ref, b_ref, o_ref, xb_ref, acc_ref):
    j = pl.program_id(1)

    @pl.when(j == 0)
    def _():
        xb_ref[...] = x_ref[...].astype(BF16)
        acc_ref[...] = jnp.zeros_like(acc_ref)

    xb = xb_ref[...]
    gate = _dot(xb, wg_ref[...])
    up = _dot(xb, wu_ref[...])
    act = (_silu(gate) * up).astype(BF16)
    acc_ref[...] += _dot(act, wd_ref[...])

    @pl.when(j == pl.num_programs(1) - 1)
    def _():
        y = DEEPNORM_ALPHA * x_ref[...] + 0.5 * acc_ref[...]
        o_ref[...] = _layer_norm(y, g_ref[...], b_ref[...])


def ffn_ln(x, w_gu, w_down, g, b, tm=FFN_TM, tf=FFN_TF):
    t, d = x.shape
    ff = w_down.shape[0]
    tm = min(tm, t)
    nf = ff // tf
    return pl.pallas_call(
        _ffn_kernel,
        grid=(t // tm, nf),
        in_specs=[
            pl.BlockSpec((tm, d), lambda i, j: (i, 0)),
            pl.BlockSpec((d, tf), lambda i, j: (0, j)),
            pl.BlockSpec((d, tf), lambda i, j: (0, j + nf)),
            pl.BlockSpec((tf, d), lambda i, j: (j, 0)),
            pl.BlockSpec((1, d), lambda i, j: (0, 0)),
            pl.BlockSpec((1, d), lambda i, j: (0, 0)),
        ],
        out_specs=pl.BlockSpec((tm, d), lambda i, j: (i, 0)),
        out_shape=jax.ShapeDtypeStruct((t, d), F32),
        scratch_shapes=[pltpu.VMEM((tm, d), BF16), pltpu.VMEM((tm, d), F32)],
        compiler_params=_cparams(("parallel", "arbitrary")),
        name="ffn_ln",
    )(x, w_gu, w_gu, w_down, g, b)


def _proj_kernel(x_ref, wk_ref, wt_ref, wc_ref, wg_ref,
                 ka_ref, kb_ref, qta_ref, vta_ref, qtb_ref, vtb_ref, c_ref, gate_ref):
    xb = x_ref[...].astype(BF16)
    k = _dot(xb, wk_ref[...])
    ka_ref[...] = k[:, :A_WIDTH].astype(BF16)
    kb_ref[...] = k[:, A_WIDTH:].astype(BF16)
    t = _dot_nt(wt_ref[...], xb)
    qta_ref[0] = (t[0:256] * (A_QK_DIM ** -0.5)).astype(BF16)
    vta_ref[0] = t[256:512].astype(BF16)
    qtb_ref[0] = (t[512:768] * (B_HEAD_DIM ** -0.5)).astype(BF16)
    vtb_ref[0] = t[768:1024].astype(BF16)
    c_ref[...] = _dot(xb, wc_ref[...])
    gate_ref[...] = _dot(xb, wg_ref[...])


def in_proj(x, wk, wt, wc, wg, tm=PROJ_TM):
    t, d = x.shape
    nt = t // tm
    full = lambda a: pl.BlockSpec(a.shape, lambda i: (0,) * a.ndim)
    row = lambda w: pl.BlockSpec((tm, w), lambda i: (i, 0))
    tr = pl.BlockSpec((1, 256, tm), lambda i: (i, 0, 0))
    tr_shape = jax.ShapeDtypeStruct((nt, 256, tm), BF16)
    return pl.pallas_call(
        _proj_kernel,
        grid=(nt,),
        in_specs=[row(d), full(wk), full(wt), full(wc), full(wg)],
        out_specs=[row(256), row(256), tr, tr, tr, tr, row(wc.shape[1]), row(V7X_LANES)],
        out_shape=[
            jax.ShapeDtypeStruct((t, 256), BF16), jax.ShapeDtypeStruct((t, 256), BF16),
            tr_shape, tr_shape, tr_shape, tr_shape,
            jax.ShapeDtypeStruct((t, wc.shape[1]), F32), jax.ShapeDtypeStruct((t, V7X_LANES), F32),
        ],
        compiler_params=_cparams(("parallel",)),
        name="in_proj",
    )(x, wk, wt, wc, wg)


def _diff_attn_kernel(q_ref, k_ref, v_ref, lam_ref, g_ref, o_ref,
                      qm_ref, m_ref, l_ref, a_ref, acc_ref, s_ref, p_ref, ot_ref, *, lambda_init, tile):
    qi = pl.program_id(1)
    nc = 2 * A_HEADS
    qt = q_ref[0]
    row = lax.broadcasted_iota(jnp.int32, qt.shape, 0)
    for c in range(nc):
        keep = (row >= c * A_QK_DIM) & (row < (c + 1) * A_QK_DIM)
        qm_ref[c] = jnp.where(keep, qt, jnp.zeros_like(qt))
    m_ref[...] = jnp.full(m_ref.shape, NEG_BIG, F32)
    l_ref[...] = jnp.zeros_like(l_ref)
    acc_ref[...] = jnp.zeros_like(acc_ref)

    def do_tile(j, masked):
        kb = k_ref[pl.ds(pl.multiple_of(j * tile, tile), tile), :]
        vt = v_ref[j]
        for c in range(nc):
            s_ref[c] = _dot(kb, qm_ref[c])
        if masked:
            key = lax.broadcasted_iota(jnp.int32, (tile, tile), 0)
            qry = lax.broadcasted_iota(jnp.int32, (tile, tile), 1)
            causal = key <= qry
        for c in range(nc):
            s = s_ref[c]
            if masked:
                s = jnp.where(causal, s, NEG_BIG)
            m_old = m_ref[c]
            m_new = jnp.maximum(m_old, jnp.max(s, axis=0, keepdims=True))
            p = jnp.exp(s - m_new)
            a = jnp.exp(m_old - m_new)
            l_ref[c] = a * l_ref[c] + jnp.sum(p, axis=0, keepdims=True)
            m_ref[c] = m_new
            a_ref[c] = a
            p_ref[c] = p.astype(BF16)
        for c in range(nc):
            h = c // 2
            pv = _dot(vt[h * A_V_DIM:(h + 1) * A_V_DIM, :], p_ref[c])
            acc_ref[c] = a_ref[c] * acc_ref[c] + pv

    def body(j, carry):
        do_tile(j, False)
        return carry

    lax.fori_loop(0, qi, body, 0)
    do_tile(qi, True)

    lf = lam_ref[...]
    lam = (jnp.exp(jnp.sum(lf[0:1] * lf[1:2], axis=-1, keepdims=True))
           - jnp.exp(jnp.sum(lf[2:3] * lf[3:4], axis=-1, keepdims=True)) + lambda_init)
    g = g_ref[...]
    for h in range(A_HEADS):
        o = acc_ref[2 * h] / l_ref[2 * h] - lam * (acc_ref[2 * h + 1] / l_ref[2 * h + 1])
        ms = jnp.mean(o * o, axis=0, keepdims=True)
        ot_ref[h * A_V_DIM:(h + 1) * A_V_DIM, :] = o * lax.rsqrt(ms + RMS_EPS) * g * (1.0 - lambda_init)
    o_ref[...] = ot_ref[...].T.astype(BF16)


def diff_attn(qt, k, vt, diff_lambda, norm_g, batch, lambda_init, tile=ATT_T):
    t = k.shape[0]
    s = t // batch
    nq = s // tile
    nc = 2 * A_HEADS
    kern = functools.partial(_diff_attn_kernel, lambda_init=lambda_init, tile=tile)
    return pl.pallas_call(
        kern,
        grid=(batch, nq),
        in_specs=[
            pl.BlockSpec((1, 256, tile), lambda b, i: (b * nq + i, 0, 0)),
            pl.BlockSpec((s, 256), lambda b, i: (b, 0)),
            pl.BlockSpec((nq, 256, tile), lambda b, i: (b, 0, 0)),
            pl.BlockSpec(diff_lambda.shape, lambda b, i: (0, 0)),
            pl.BlockSpec((A_V_DIM, 1), lambda b, i: (0, 0)),
        ],
        out_specs=pl.BlockSpec((tile, 256), lambda b, i: (b * nq + i, 0)),
        out_shape=jax.ShapeDtypeStruct((t, 256), BF16),
        scratch_shapes=[
            pltpu.VMEM((nc, 256, tile), BF16),
            pltpu.VMEM((nc, 1, tile), F32),
            pltpu.VMEM((nc, 1, tile), F32),
            pltpu.VMEM((nc, 1, tile), F32),
            pltpu.VMEM((nc, A_V_DIM, tile), F32),
            pltpu.VMEM((nc, tile, tile), F32),
            pltpu.VMEM((nc, tile, tile), BF16),
            pltpu.VMEM((256, tile), F32),
        ],
        compiler_params=_cparams(("parallel", "arbitrary")),
        name="diff_attn",
    )(qt, k, vt, diff_lambda, norm_g.reshape(A_V_DIM, 1))


def _stick_attn_kernel(q_ref, k_ref, v_ref, g_ref, o_ref,
                       qm_ref, carry_ref, acc_ref, z_ref, hl_ref, t_ref, w_ref, ot_ref, *, tile):
    qi = pl.program_id(1)
    qt = q_ref[0]
    row = lax.broadcasted_iota(jnp.int32, qt.shape, 0)
    for h in range(B_HEADS):
        keep = (row >= h * B_HEAD_DIM) & (row < (h + 1) * B_HEAD_DIM)
        qm_ref[h] = jnp.where(keep, qt, jnp.zeros_like(qt))
    carry_ref[...] = jnp.zeros_like(carry_ref)
    acc_ref[...] = jnp.zeros_like(acc_ref)

    key = lax.broadcasted_iota(jnp.int32, (tile, tile), 0)
    qry = lax.broadcasted_iota(jnp.int32, (tile, tile), 1)
    upper = (qry > key).astype(BF16)
    upper2 = jnp.concatenate([upper, upper], axis=1)
    strict = key < qry

    def do_tile(j, masked):
        kb = k_ref[pl.ds(pl.multiple_of(j * tile, tile), tile), :]
        vt = v_ref[j]
        for h in range(B_HEADS):
            z_ref[h] = _dot(kb, qm_ref[h])
        for h in range(B_HEADS):
            z = z_ref[h]
            sp = jnp.maximum(z, 0.0) + jnp.log(1.0 + jnp.exp(-jnp.abs(z)))
            if masked:
                sp = jnp.where(strict, sp, 0.0)
            hi = sp.astype(BF16)
            hl_ref[h, 0:tile, :] = hi
            hl_ref[h, tile:, :] = (sp - hi.astype(F32)).astype(BF16)
            carry = carry_ref[h]
            z_ref[h] = z - sp - carry
            carry_ref[h] = carry + jnp.sum(sp, axis=0, keepdims=True)
        for h in range(B_HEADS):
            t_ref[h] = _dot(upper2, hl_ref[h])
        for h in range(B_HEADS):
            w = jnp.exp(z_ref[h] - t_ref[h])
            if masked:
                w = jnp.where(strict, w, 0.0)
            w_ref[h] = w.astype(BF16)
        for h in range(B_HEADS):
            acc_ref[h] += _dot(vt[h * B_HEAD_DIM:(h + 1) * B_HEAD_DIM, :], w_ref[h])

    do_tile(qi, True)

    def live():
        return jnp.min(carry_ref[...]) < STICK_DEAD_LOG

    def cond(state):
        jj, go = state
        return jnp.logical_and(jj < qi, go)

    def body(state):
        jj, _ = state
        do_tile(qi - 1 - jj, False)
        return jj + 1, live()

    lax.while_loop(cond, body, (jnp.int32(0), live()))

    g = g_ref[...]
    for h in range(B_HEADS):
        o = acc_ref[h]
        ms = jnp.mean(o * o, axis=0, keepdims=True)
        ot_ref[h * B_HEAD_DIM:(h + 1) * B_HEAD_DIM, :] = o * lax.rsqrt(ms + RMS_EPS) * g
    o_ref[...] = ot_ref[...].T.astype(BF16)


def stick_attn(qt, k, vt, norm_g, batch, tile=ATT_T):
    t = k.shape[0]
    s = t // batch
    nq = s // tile
    kern = functools.partial(_stick_attn_kernel, tile=tile)
    return pl.pallas_call(
        kern,
        grid=(batch, nq),
        in_specs=[
            pl.BlockSpec((1, 256, tile), lambda b, i: (b * nq + i, 0, 0)),
            pl.BlockSpec((s, 256), lambda b, i: (b, 0)),
            pl.BlockSpec((nq, 256, tile), lambda b, i: (b, 0, 0)),
            pl.BlockSpec((B_HEAD_DIM, 1), lambda b, i: (0, 0)),
        ],
        out_specs=pl.BlockSpec((tile, 256), lambda b, i: (b * nq + i, 0)),
        out_shape=jax.ShapeDtypeStruct((t, 256), BF16),
        scratch_shapes=[
            pltpu.VMEM((B_HEADS, 256, tile), BF16),
            pltpu.VMEM((B_HEADS, 1, tile), F32),
            pltpu.VMEM((B_HEADS, B_HEAD_DIM, tile), F32),
            pltpu.VMEM((B_HEADS, tile, tile), F32),
            pltpu.VMEM((B_HEADS, 2 * tile, tile), BF16),
            pltpu.VMEM((B_HEADS, tile, tile), F32),
            pltpu.VMEM((B_HEADS, tile, tile), BF16),
            pltpu.VMEM((256, tile), F32),
        ],
        compiler_params=_cparams(("parallel", "arbitrary")),
        name="stick_attn",
    )(qt, k, vt, norm_g.reshape(B_HEAD_DIM, 1))


def _gdn_prep_kernel(x_ref, halo_ref, gate_ref, cw_ref, alog_ref, dtb_ref,
                     u_ref, w_ref, qg_ref, intra_ref, kdt_ref, eg_ref, xs_ref, *, chunk):
    ci = pl.program_id(1)
    c = chunk
    dh = C_HEAD_DIM
    heads = range(C_HEADS)
    halo = jnp.where(ci > 0, halo_ref[...], 0.0)
    xs_ref[0:V7X_SUBLANES, :] = halo
    xs_ref[V7X_SUBLANES:, :] = x_ref[...]
    cw = cw_ref[...]
    conv = cw[0:1] * xs_ref[pl.ds(V7X_SUBLANES - 3, c), :]
    for i in range(1, CONV_K):
        conv = conv + cw[i:i + 1] * xs_ref[pl.ds(V7X_SUBLANES - 3 + i, c), :]
    y = _silu(conv)

    gates = gate_ref[...]
    beta_all = 1.0 / (1.0 + jnp.exp(-gates))
    g_all = -jnp.exp(alog_ref[...]) * _softplus(gates + dtb_ref[...])

    r = lax.broadcasted_iota(jnp.int32, (c, c), 0)
    q_ = lax.broadcasted_iota(jnp.int32, (c, c), 1)
    lo_incl = q_ <= r
    lo_strict = q_ < r
    lo_incl_b = lo_incl.astype(BF16)
    lo_incl2 = jnp.concatenate([lo_incl_b, lo_incl_b], axis=1)
    lo_strict_f = lo_strict.astype(F32)
    scale = dh ** -0.5

    qn, kn, kbeta, vbeta, gwide = [], [], [], [], []
    for h in heads:
        q = y[:, h * dh:(h + 1) * dh]
        k = y[:, C_WIDTH + h * dh:C_WIDTH + (h + 1) * dh]
        v = y[:, 2 * C_WIDTH + h * dh:2 * C_WIDTH + (h + 1) * dh]
        qn.append(q * lax.rsqrt(jnp.sum(q * q, axis=-1, keepdims=True) + RMS_EPS))
        k = k * lax.rsqrt(jnp.sum(k * k, axis=-1, keepdims=True) + RMS_EPS)
        kn.append(k)
        beta_b = jnp.broadcast_to(beta_all[:, h:h + 1], (c, dh))
        kbeta.append(k * beta_b)
        vbeta.append(v * beta_b)
        gwide.append(jnp.broadcast_to(g_all[:, C_HEADS + h:C_HEADS + h + 1], (c, c)))

    gc = [_dot(lo_incl2, _split2_rows(gwide[h][:, :dh])) for h in heads]
    gdiff = [_dot(lo_incl2, _split2_rows(gwide[h] * lo_strict_f)) for h in heads]
    kf = [kn[h].astype(BF16) for h in heads]
    akk = [_dot_nt(kbeta[h].astype(BF16), kf[h]) for h in heads]
    aqk = [_dot_nt(qn[h].astype(BF16), kf[h]) for h in heads]
    lmat, ymat = [], []
    for h in heads:
        decay = jnp.where(lo_incl, jnp.exp(gdiff[h]), 0.0)
        lm = jnp.where(lo_strict, akk[h] * decay, 0.0)
        lmat.append(lm)
        intra_ref[:, h * c:(h + 1) * c] = (jnp.where(lo_incl, aqk[h] * decay, 0.0) * scale).astype(BF16)
        ymat.append(-jnp.where((r >> 1) == (q_ >> 1), lm, 0.0))

    bs = 2
    while bs < c:
        sh = bs.bit_length() - 1
        join = ((r >> (sh + 1)) == (q_ >> (sh + 1))) & ((r >> sh) != (q_ >> sh))
        cb = [jnp.where(join, lmat[h], 0.0) for h in heads]
        yb = [ymat[h].astype(BF16) for h in heads]
        p = [cb[h] + _dot(yb[h], cb[h].astype(BF16)) for h in heads]
        ymat = [ymat[h] - p[h] - _dot(p[h].astype(BF16), yb[h]) for h in heads]
        bs *= 2

    egc = [jnp.exp(gc[h]) for h in heads]
    rhs = [jnp.concatenate([vbeta[h], kbeta[h] * egc[h]], axis=1) for h in heads]
    sol = [rhs[h] + _dot(ymat[h].astype(BF16), rhs[h].astype(BF16)) for h in heads]
    for h in heads:
        sl = slice(h * dh, (h + 1) * dh)
        total = gc[h][c - 1:c, :]
        u_ref[:, sl] = sol[h][:, :dh]
        w_ref[:, sl] = sol[h][:, dh:].astype(BF16)
        qg_ref[:, sl] = (qn[h] * egc[h] * scale).astype(BF16)
        kd = kn[h] * jnp.exp(total - gc[h])
        kdt_ref[0, sl, :] = kd.T.astype(BF16)
        eg_ref[0, :, sl] = jnp.broadcast_to(jnp.exp(total), (V7X_SUBLANES, dh))


def gdn_prep(cbuf, gates, conv_w, alog_row, dtb_row, batch, chunk=GDN_C):
    t = cbuf.shape[0]
    s = t // batch
    nc = s // chunk
    hb = chunk // V7X_SUBLANES
    w3 = 3 * C_WIDTH
    kern = functools.partial(_gdn_prep_kernel, chunk=chunk)
    row = lambda w: pl.BlockSpec((chunk, w), lambda b, i: (b * nc + i, 0))
    return pl.pallas_call(
        kern,
        grid=(batch, nc),
        in_specs=[
            row(w3),
            pl.BlockSpec((V7X_SUBLANES, w3), lambda b, i: (jnp.maximum((b * nc + i) * hb - 1, 0), 0)),
            row(V7X_LANES),
            pl.BlockSpec(conv_w.shape, lambda b, i: (0, 0)),
            pl.BlockSpec((1, V7X_LANES), lambda b, i: (0, 0)),
            pl.BlockSpec((1, V7X_LANES), lambda b, i: (0, 0)),
        ],
        out_specs=[
            row(C_WIDTH), row(C_WIDTH), row(C_WIDTH), row(C_HEADS * chunk),
            pl.BlockSpec((1, C_WIDTH, chunk), lambda b, i: (b * nc + i, 0, 0)),
            pl.BlockSpec((1, V7X_SUBLANES, C_WIDTH), lambda b, i: (b * nc + i, 0, 0)),
        ],
        out_shape=[
            jax.ShapeDtypeStruct((t, C_WIDTH), F32),
            jax.ShapeDtypeStruct((t, C_WIDTH), BF16),
            jax.ShapeDtypeStruct((t, C_WIDTH), BF16),
            jax.ShapeDtypeStruct((t, C_HEADS * chunk), BF16),
            jax.ShapeDtypeStruct((t // chunk, C_WIDTH, chunk), BF16),
            jax.ShapeDtypeStruct((t // chunk, V7X_SUBLANES, C_WIDTH), F32),
        ],
        scratch_shapes=[pltpu.VMEM((chunk + V7X_SUBLANES, w3), F32)],
        compiler_params=_cparams(("parallel", "arbitrary")),
        name="gdn_prep",
    )(cbuf, cbuf, gates, conv_w, alog_row, dtb_row)


def _gdn_seq_kernel(u_ref, w_ref, qg_ref, intra_ref, kdt_ref, eg_ref, z_ref, gn_ref, o_ref, s_ref, *, chunk):
    @pl.when(pl.program_id(1) == 0)
    def _():
        s_ref[...] = jnp.zeros_like(s_ref)

    c = chunk
    dh = C_HEAD_DIM
    heads = range(C_HEADS)
    sls = [slice(h * dh, (h + 1) * dh) for h in heads]
    gn = gn_ref[...]
    state = [s_ref[h] for h in heads]
    sb = [state[h].astype(BF16) for h in heads]
    ws = [_dot(w_ref[:, sls[h]], sb[h]) for h in heads]
    qs = [_dot(qg_ref[:, sls[h]], sb[h]) for h in heads]
    vnb = [(u_ref[:, sls[h]] - ws[h]).astype(BF16) for h in heads]
    o = [qs[h] + _dot(intra_ref[:, h * c:(h + 1) * c], vnb[h]) for h in heads]
    for h in heads:
        s_ref[h] = state[h] * eg_ref[0, 0:1, sls[h]] + _dot(kdt_ref[0, sls[h], :], vnb[h])
    for h in heads:
        on = o[h] * lax.rsqrt(jnp.mean(o[h] * o[h], axis=-1, keepdims=True) + RMS_EPS) * gn
        o_ref[:, sls[h]] = (on * _silu(z_ref[:, sls[h]])).astype(BF16)


def gdn_seq(u, w, qg, intra, kdt, eg, cbuf, norm_g, batch, chunk=GDN_C):
    t = u.shape[0]
    nc = t // batch // chunk
    row = pl.BlockSpec((chunk, C_WIDTH), lambda b, i: (b * nc + i, 0))
    return pl.pallas_call(
        functools.partial(_gdn_seq_kernel, chunk=chunk),
        grid=(batch, nc),
        in_specs=[
            row, row, row,
            pl.BlockSpec((chunk, C_HEADS * chunk), lambda b, i: (b * nc + i, 0)),
            pl.BlockSpec((1, C_WIDTH, chunk), lambda b, i: (b * nc + i, 0, 0)),
            pl.BlockSpec((1, V7X_SUBLANES, C_WIDTH), lambda b, i: (b * nc + i, 0, 0)),
            pl.BlockSpec((chunk, C_WIDTH), lambda b, i: (b * nc + i, 3)),
            pl.BlockSpec((1, C_HEAD_DIM), lambda b, i: (0, 0)),
        ],
        out_specs=row,
        out_shape=jax.ShapeDtypeStruct((t, C_WIDTH), BF16),
        scratch_shapes=[pltpu.VMEM((C_HEADS, C_HEAD_DIM, C_HEAD_DIM), F32)],
        compiler_params=_cparams(("parallel", "arbitrary")),
        name="gdn_seq",
    )(u, w, qg, intra, kdt, eg, cbuf, norm_g.reshape(1, C_HEAD_DIM))


def _out_kernel(x_ref, oa_ref, ob_ref, oc_ref, w_ref, g_ref, b_ref, o_ref):
    mix = _dot(oa_ref[...], w_ref[0:A_WIDTH, :])
    mix = mix + _dot(ob_ref[...], w_ref[A_WIDTH:A_WIDTH + B_WIDTH, :])
    mix = mix + _dot(oc_ref[...], w_ref[A_WIDTH + B_WIDTH:, :])
    y = DEEPNORM_ALPHA * x_ref[...] + mix
    o_ref[...] = _layer_norm(y, g_ref[...], b_ref[...])


def out_ln(x, oa, ob, oc, w_out, g, b, tm=OUT_TM):
    t, d = x.shape
    tm = min(tm, t)
    row = lambda w: pl.BlockSpec((tm, w), lambda i: (i, 0))
    return pl.pallas_call(
        _out_kernel,
        grid=(t // tm,),
        in_specs=[row(d), row(A_WIDTH), row(B_WIDTH), row(C_WIDTH),
                  pl.BlockSpec(w_out.shape, lambda i: (0, 0)),
                  pl.BlockSpec((1, d), lambda i: (0, 0)), pl.BlockSpec((1, d), lambda i: (0, 0))],
        out_specs=row(d),
        out_shape=jax.ShapeDtypeStruct((t, d), F32),
        compiler_params=_cparams(("parallel",)),
        name="out_ln",
    )(x, oa, ob, oc, w_out, g, b)


def _split_w_in(w):
    o = 0
    parts = {}
    for name, n in (("qa", 256), ("ka", 256), ("va", 256), ("qb", 256), ("kb", 256), ("vb", 256),
                    ("qkv", 3 * C_WIDTH), ("z", C_WIDTH), ("beta", C_HEADS), ("a", C_HEADS)):
        parts[name] = w[:, o:o + n]
        o += n
    wk = jnp.concatenate([parts["ka"], parts["kb"]], axis=1).astype(BF16)
    wt = jnp.concatenate([parts["qa"], parts["va"], parts["qb"], parts["vb"]], axis=1).T.astype(BF16)
    wc = jnp.concatenate([parts["qkv"], parts["z"]], axis=1).astype(BF16)
    pad = jnp.zeros((w.shape[0], V7X_LANES - 2 * C_HEADS), w.dtype)
    wg = jnp.concatenate([parts["beta"], parts["a"], pad], axis=1).astype(BF16)
    return wk, wt, wc, wg


def _gate_row(vals):
    row = jnp.zeros((1, V7X_LANES), F32)
    return row.at[0, C_HEADS:2 * C_HEADS].set(vals.astype(F32))


def kernel(x, ffn1_w_gu, ffn1_w_down, ffn2_w_gu, ffn2_w_down, ln_g, ln_b, w_in, conv_w, dn_a_log, dn_dt_bias,
           dn_norm_g, diff_lambda, diff_norm_g, sb_norm_g, w_out):
    batch, seq, d = x.shape
    h = x.reshape(batch * seq, d)
    for l in range(DEPTH):
        lambda_init = 0.8 - 0.6 * math.exp(-0.3 * l)
        ln = lambda i: (ln_g[l, i].reshape(1, d), ln_b[l, i].reshape(1, d))
        h = ffn_ln(h, ffn1_w_gu[l].astype(BF16), ffn1_w_down[l].astype(BF16), *ln(0))
        wk, wt, wc, wg = _split_w_in(w_in[l])
        ka, kb, qta, vta, qtb, vtb, cbuf, gates = in_proj(h, wk, wt, wc, wg)
        oa = diff_attn(qta, ka, vta, diff_lambda[l], diff_norm_g[l], batch, lambda_init)
        ob = stick_attn(qtb, kb, vtb, sb_norm_g[l], batch)
        u, w, qg, intra, kdt, eg = gdn_prep(cbuf, gates, conv_w[l], _gate_row(dn_a_log[l]),
                                            _gate_row(dn_dt_bias[l]), batch)
        oc = gdn_seq(u, w, qg, intra, kdt, eg, cbuf, dn_norm_g[l], batch)
        h = out_ln(h, oa, ob, oc, w_out[l].astype(BF16), *ln(1))
        h = ffn_ln(h, ffn2_w_gu[l].astype(BF16), ffn2_w_down[l].astype(BF16), *ln(2))
    return h.reshape(batch, seq, d)
```

```python
import functools
import math

import jax
import jax.numpy as jnp
from jax import lax
from jax.experimental import pallas as pl
from jax.experimental.pallas import tpu as pltpu

F32 = jnp.float32
BF16 = jnp.bfloat16

DEPTH = 2
A_HEADS, A_QK_DIM, A_V_DIM = 4, 32, 64
B_HEADS, B_HEAD_DIM = 4, 64
C_HEADS, C_HEAD_DIM = 4, 128
CONV_K = 4
D_FF = 2816
A_WIDTH, B_WIDTH, C_WIDTH = 256, 256, 512
DEEPNORM_ALPHA = (2.0 * DEPTH) ** 0.25
LN_EPS = 1e-5
RMS_EPS = 1e-6
NEG_BIG = -1e30
LOG2_E = math.log2(math.e)
STICK_DEAD_LOG = 104.0

V7X_LANES = 128
V7X_SUBLANES = 8
V7X_VMEM_LIMIT_BYTES = 56 * 1024 * 1024
ONES_ROWS = 16

FFN_TM = 1024
FFN_TF = 256
PROJ_TM = 256
ATT_T = 256
GDN_C = 256
OUT_TM = 512


def _cparams(sem):
    return pltpu.CompilerParams(dimension_semantics=sem, vmem_limit_bytes=V7X_VMEM_LIMIT_BYTES)


def _layer_norm(y, g, b):
    mu = jnp.mean(y, axis=-1, keepdims=True)
    d = y - mu
    var = jnp.mean(d * d, axis=-1, keepdims=True)
    return d * lax.rsqrt(var + LN_EPS) * g + b


def _sigmoid(x):
    return 0.5 * jnp.tanh(0.5 * x) + 0.5


def _silu(x):
    return x * _sigmoid(x)


def _softplus(x):
    return jnp.maximum(x, 0.0) + jnp.log1p(jnp.exp(-jnp.abs(x)))


def _dot(a, b):
    return jnp.dot(a, b, preferred_element_type=F32)


def _dot_nt(a, b):
    return lax.dot_general(a, b, (((1,), (1,)), ((), ())), preferred_element_type=F32)


def _split2_rows(x):
    hi = x.astype(BF16)
    lo = (x - hi.astype(F32)).astype(BF16)
    return jnp.concatenate([hi, lo], axis=0)


def _ffn_kernel(x_ref, wg_ref, wu_ref, wd_ref, g_ref, b_ref, o_ref, xb_ref, acc_ref):
    j = pl.program_id(1)

    @pl.when(j == 0)
    def _():
        xb_ref[...] = x_ref[...].astype(BF16)
        acc_ref[...] = jnp.zeros_like(acc_ref)

    xb = xb_ref[...]
    gate = _dot(xb, wg_ref[...])
    up = _dot(xb, wu_ref[...])
    act = (_silu(gate) * up).astype(BF16)
    acc_ref[...] += _dot(act, wd_ref[...])

    @pl.when(j == pl.num_programs(1) - 1)
    def _():
        y = DEEPNORM_ALPHA * x_ref[...] + 0.5 * acc_ref[...]
        o_ref[...] = _layer_norm(y, g_ref[...], b_ref[...])


def ffn_ln(x, w_gu, w_down, g, b, tm=FFN_TM, tf=FFN_TF):
    t, d = x.shape
    ff = w_down.shape[0]
    tm = min(tm, t)
    nf = ff // tf
    return pl.pallas_call(
        _ffn_kernel,
        grid=(t // tm, nf),
        in_specs=[
            pl.BlockSpec((tm, d), lambda i, j: (i, 0)),
            pl.BlockSpec((d, tf), lambda i, j: (0, j)),
            pl.BlockSpec((d, tf), lambda i, j: (0, j + nf)),
            pl.BlockSpec((tf, d), lambda i, j: (j, 0)),
            pl.BlockSpec((1, d), lambda i, j: (0, 0)),
            pl.BlockSpec((1, d), lambda i, j: (0, 0)),
        ],
        out_specs=pl.BlockSpec((tm, d), lambda i, j: (i, 0)),
        out_shape=jax.ShapeDtypeStruct((t, d), F32),
        scratch_shapes=[pltpu.VMEM((tm, d), BF16), pltpu.VMEM((tm, d), F32)],
        compiler_params=_cparams(("parallel", "arbitrary")),
        name="ffn_ln",
    )(x, w_gu, w_gu, w_down, g, b)


def _proj_kernel(x_ref, wk_ref, wt_ref, wc_ref, wg_ref,
                 ka_ref, kb_ref, qta_ref, vta_ref, qtb_ref, vtb_ref, c_ref, gate_ref):
    xb = x_ref[...].astype(BF16)
    k = _dot(xb, wk_ref[...])
    ka_ref[...] = k[:, :A_WIDTH].astype(BF16)
    kb_ref[...] = k[:, A_WIDTH:].astype(BF16)
    t = _dot_nt(wt_ref[...], xb)
    qta_ref[0] = (t[0:256] * (A_QK_DIM ** -0.5 * LOG2_E)).astype(BF16)
    vta_ref[0] = t[256:512].astype(BF16)
    qtb_ref[0] = (t[512:768] * (B_HEAD_DIM ** -0.5)).astype(BF16)
    vtb_ref[0] = t[768:1024].astype(BF16)
    c_ref[...] = _dot(xb, wc_ref[...])
    gate_ref[...] = _dot(xb, wg_ref[...])


def in_proj(x, wk, wt, wc, wg, tm=PROJ_TM):
    t, d = x.shape
    nt = t // tm
    full = lambda a: pl.BlockSpec(a.shape, lambda i: (0,) * a.ndim)
    row = lambda w: pl.BlockSpec((tm, w), lambda i: (i, 0))
    tr = pl.BlockSpec((1, 256, tm), lambda i: (i, 0, 0))
    tr_shape = jax.ShapeDtypeStruct((nt, 256, tm), BF16)
    return pl.pallas_call(
        _proj_kernel,
        grid=(nt,),
        in_specs=[row(d), full(wk), full(wt), full(wc), full(wg)],
        out_specs=[row(256), row(256), tr, tr, tr, tr, row(wc.shape[1]), row(V7X_LANES)],
        out_shape=[
            jax.ShapeDtypeStruct((t, 256), BF16), jax.ShapeDtypeStruct((t, 256), BF16),
            tr_shape, tr_shape, tr_shape, tr_shape,
            jax.ShapeDtypeStruct((t, wc.shape[1]), F32), jax.ShapeDtypeStruct((t, V7X_LANES), F32),
        ],
        compiler_params=_cparams(("parallel",)),
        name="in_proj",
    )(x, wk, wt, wc, wg)


def _diff_attn_kernel(q_ref, k_ref, v_ref, lam_ref, g_ref, o_ref,
                      qm_ref, m_ref, a_ref, acc_ref, s_ref, tm_ref, p_ref, ot_ref, *, lambda_init, tile):
    qi = pl.program_id(1)
    nc = 2 * A_HEADS
    qt = q_ref[0]
    row = lax.broadcasted_iota(jnp.int32, qt.shape, 0)
    for c in range(nc):
        keep = (row >= c * A_QK_DIM) & (row < (c + 1) * A_QK_DIM)
        qm_ref[c] = jnp.where(keep, qt, jnp.zeros_like(qt))
    m_ref[...] = jnp.full(m_ref.shape, NEG_BIG, F32)
    acc_ref[...] = jnp.zeros_like(acc_ref)
    ones = jnp.ones((ONES_ROWS, tile), BF16)

    def scores(j, slot):
        kb = k_ref[pl.ds(pl.multiple_of(j * tile, tile), tile), :]
        for c in range(nc):
            s = _dot(kb, qm_ref[c])
            s_ref[slot, c] = s
            tm_ref[slot, c] = jnp.max(s, axis=0, keepdims=True)

    def softmax_pv(j, slot, masked):
        vt = v_ref[j]
        if masked:
            key = lax.broadcasted_iota(jnp.int32, (tile, tile), 0)
            qry = lax.broadcasted_iota(jnp.int32, (tile, tile), 1)
            causal = key <= qry
        m_new = []
        for c in range(nc):
            if masked:
                tmax = jnp.max(jnp.where(causal, s_ref[slot, c], NEG_BIG), axis=0, keepdims=True)
            else:
                tmax = tm_ref[slot, c]
            m_old = m_ref[c]
            m_new.append(jnp.maximum(m_old, tmax))
            a_ref[c] = jnp.exp2(m_old - m_new[c])
            m_ref[c] = m_new[c]
        for c in range(nc):
            s = s_ref[slot, c]
            if masked:
                s = jnp.where(causal, s, NEG_BIG)
            p_ref[c] = jnp.exp2(s - m_new[c]).astype(BF16)
        for c in range(nc):
            h = c // 2
            vt_ext = jnp.concatenate([vt[h * A_V_DIM:(h + 1) * A_V_DIM, :], ones], axis=0)
            acc_ref[c] = a_ref[c] * acc_ref[c] + _dot(vt_ext, p_ref[c])

    scores(0, 0)

    def body(jj, carry):
        j = 2 * jj
        scores(j + 1, 1)
        softmax_pv(j, 0, False)
        scores(j + 2, 0)
        softmax_pv(j + 1, 1, False)
        return carry

    lax.fori_loop(0, qi // 2, body, 0)

    @pl.when(qi % 2 == 0)
    def _():
        softmax_pv(qi, 0, True)

    @pl.when(qi % 2 == 1)
    def _():
        scores(qi, 1)
        softmax_pv(qi - 1, 0, False)
        softmax_pv(qi, 1, True)

    lf = lam_ref[...]
    lam = (jnp.exp(jnp.sum(lf[0:1] * lf[1:2], axis=-1, keepdims=True))
           - jnp.exp(jnp.sum(lf[2:3] * lf[3:4], axis=-1, keepdims=True)) + lambda_init)
    g = g_ref[...]
    dv = A_V_DIM
    for h in range(A_HEADS):
        a1 = acc_ref[2 * h]
        a2 = acc_ref[2 * h + 1]
        o = a1[:dv] / a1[dv:dv + 1] - lam * (a2[:dv] / a2[dv:dv + 1])
        ms = jnp.mean(o * o, axis=0, keepdims=True)
        ot_ref[h * A_V_DIM:(h + 1) * A_V_DIM, :] = o * lax.rsqrt(ms + RMS_EPS) * g * (1.0 - lambda_init)
    o_ref[...] = ot_ref[...].T.astype(BF16)


def diff_attn(qt, k, vt, diff_lambda, norm_g, batch, lambda_init, tile=ATT_T):
    t = k.shape[0]
    s = t // batch
    nq = s // tile
    nc = 2 * A_HEADS
    kern = functools.partial(_diff_attn_kernel, lambda_init=lambda_init, tile=tile)
    return pl.pallas_call(
        kern,
        grid=(batch, nq),
        in_specs=[
            pl.BlockSpec((1, 256, tile), lambda b, i: (b * nq + i, 0, 0)),
            pl.BlockSpec((s, 256), lambda b, i: (b, 0)),
            pl.BlockSpec((nq, 256, tile), lambda b, i: (b, 0, 0)),
            pl.BlockSpec(diff_lambda.shape, lambda b, i: (0, 0)),
            pl.BlockSpec((A_V_DIM, 1), lambda b, i: (0, 0)),
        ],
        out_specs=pl.BlockSpec((tile, 256), lambda b, i: (b * nq + i, 0)),
        out_shape=jax.ShapeDtypeStruct((t, 256), BF16),
        scratch_shapes=[
            pltpu.VMEM((nc, 256, tile), BF16),
            pltpu.VMEM((nc, 1, tile), F32),
            pltpu.VMEM((nc, 1, tile), F32),
            pltpu.VMEM((nc, A_V_DIM + ONES_ROWS, tile), F32),
            pltpu.VMEM((2, nc, tile, tile), F32),
            pltpu.VMEM((2, nc, 1, tile), F32),
            pltpu.VMEM((nc, tile, tile), BF16),
            pltpu.VMEM((256, tile), F32),
        ],
        compiler_params=_cparams(("parallel", "arbitrary")),
        name="diff_attn",
    )(qt, k, vt, diff_lambda, norm_g.reshape(A_V_DIM, 1))


def _stick_attn_kernel(q_ref, k_ref, v_ref, g_ref, o_ref,
                       qm_ref, carry_ref, acc_ref, z_ref, hl_ref, t_ref, w_ref, ot_ref, *, tile):
    qi = pl.program_id(1)
    qt = q_ref[0]
    row = lax.broadcasted_iota(jnp.int32, qt.shape, 0)
    for h in range(B_HEADS):
        keep = (row >= h * B_HEAD_DIM) & (row < (h + 1) * B_HEAD_DIM)
        qm_ref[h] = jnp.where(keep, qt, jnp.zeros_like(qt))
    carry_ref[...] = jnp.zeros_like(carry_ref)
    acc_ref[...] = jnp.zeros_like(acc_ref)

    key = lax.broadcasted_iota(jnp.int32, (tile, tile), 0)
    qry = lax.broadcasted_iota(jnp.int32, (tile, tile), 1)
    upper = (qry > key).astype(BF16)
    upper2 = jnp.concatenate([upper, upper], axis=1)
    strict = key < qry

    def do_tile(j, masked):
        kb = k_ref[pl.ds(pl.multiple_of(j * tile, tile), tile), :]
        vt = v_ref[j]
        for h in range(B_HEADS):
            z_ref[h] = _dot(kb, qm_ref[h])
        for h in range(B_HEADS):
            z = z_ref[h]
            sp = jnp.maximum(z, 0.0) + jnp.log(1.0 + jnp.exp(-jnp.abs(z)))
            if masked:
                sp = jnp.where(strict, sp, 0.0)
            hi = sp.astype(BF16)
            hl_ref[h, 0:tile, :] = hi
            hl_ref[h, tile:, :] = (sp - hi.astype(F32)).astype(BF16)
            carry = carry_ref[h]
            z_ref[h] = z - sp - carry
            carry_ref[h] = carry + jnp.sum(sp, axis=0, keepdims=True)
        for h in range(B_HEADS):
            t_ref[h] = _dot(upper2, hl_ref[h])
        for h in range(B_HEADS):
            w = jnp.exp(z_ref[h] - t_ref[h])
            if masked:
                w = jnp.where(strict, w, 0.0)
            w_ref[h] = w.astype(BF16)
        for h in range(B_HEADS):
            acc_ref[h] += _dot(vt[h * B_HEAD_DIM:(h + 1) * B_HEAD_DIM, :], w_ref[h])

    do_tile(qi, True)

    def live():
        return jnp.min(carry_ref[...]) < STICK_DEAD_LOG

    def cond(state):
        jj, go = state
        return jnp.logical_and(jj < qi, go)

    def body(state):
        jj, _ = state
        do_tile(qi - 1 - jj, False)
        return jj + 1, live()

    lax.while_loop(cond, body, (jnp.int32(0), live()))

    g = g_ref[...]
    for h in range(B_HEADS):
        o = acc_ref[h]
        ms = jnp.mean(o * o, axis=0, keepdims=True)
        ot_ref[h * B_HEAD_DIM:(h + 1) * B_HEAD_DIM, :] = o * lax.rsqrt(ms + RMS_EPS) * g
    o_ref[...] = ot_ref[...].T.astype(BF16)


def stick_attn(qt, k, vt, norm_g, batch, tile=ATT_T):
    t = k.shape[0]
    s = t // batch
    nq = s // tile
    kern = functools.partial(_stick_attn_kernel, tile=tile)
    return pl.pallas_call(
        kern,
        grid=(batch, nq),
        in_specs=[
            pl.BlockSpec((1, 256, tile), lambda b, i: (b * nq + i, 0, 0)),
            pl.BlockSpec((s, 256), lambda b, i: (b, 0)),
            pl.BlockSpec((nq, 256, tile), lambda b, i: (b, 0, 0)),
            pl.BlockSpec((B_HEAD_DIM, 1), lambda b, i: (0, 0)),
        ],
        out_specs=pl.BlockSpec((tile, 256), lambda b, i: (b * nq + i, 0)),
        out_shape=jax.ShapeDtypeStruct((t, 256), BF16),
        scratch_shapes=[
            pltpu.VMEM((B_HEADS, 256, tile), BF16),
            pltpu.VMEM((B_HEADS, 1, tile), F32),
            pltpu.VMEM((B_HEADS, B_HEAD_DIM, tile), F32),
            pltpu.VMEM((B_HEADS, tile, tile), F32),
            pltpu.VMEM((B_HEADS, 2 * tile, tile), BF16),
            pltpu.VMEM((B_HEADS, tile, tile), F32),
            pltpu.VMEM((B_HEADS, tile, tile), BF16),
            pltpu.VMEM((256, tile), F32),
        ],
        compiler_params=_cparams(("parallel", "arbitrary")),
        name="stick_attn",
    )(qt, k, vt, norm_g.reshape(B_HEAD_DIM, 1))


def _gdn_prep_kernel(x_ref, halo_ref, gate_ref, cw_ref, alog_ref, dtb_ref,
                     u_ref, w_ref, qg_ref, intra_ref, kdt_ref, eg_ref, xs_ref, *, chunk):
    ci = pl.program_id(1)
    c = chunk
    dh = C_HEAD_DIM
    heads = range(C_HEADS)
    halo = jnp.where(ci > 0, halo_ref[...], 0.0)
    xs_ref[0:V7X_SUBLANES, :] = halo
    xs_ref[V7X_SUBLANES:, :] = x_ref[...]
    cw = cw_ref[...]
    conv = cw[0:1] * xs_ref[pl.ds(V7X_SUBLANES - 3, c), :]
    for i in range(1, CONV_K):
        conv = conv + cw[i:i + 1] * xs_ref[pl.ds(V7X_SUBLANES - 3 + i, c), :]
    y = _silu(conv)

    gates = gate_ref[...]
    beta_all = _sigmoid(gates)
    g_all = -jnp.exp(alog_ref[...]) * _softplus(gates + dtb_ref[...])

    r = lax.broadcasted_iota(jnp.int32, (c, c), 0)
    q_ = lax.broadcasted_iota(jnp.int32, (c, c), 1)
    lo_incl = q_ <= r
    lo_strict = q_ < r
    lo_incl_b = lo_incl.astype(BF16)
    lo_incl2 = jnp.concatenate([lo_incl_b, lo_incl_b], axis=1)
    lo_strict_f = lo_strict.astype(F32)
    scale = dh ** -0.5

    qn, kn, kbeta, vbeta, gwide = [], [], [], [], []
    for h in heads:
        q = y[:, h * dh:(h + 1) * dh]
        k = y[:, C_WIDTH + h * dh:C_WIDTH + (h + 1) * dh]
        v = y[:, 2 * C_WIDTH + h * dh:2 * C_WIDTH + (h + 1) * dh]
        qn.append(q * lax.rsqrt(jnp.sum(q * q, axis=-1, keepdims=True) + RMS_EPS))
        k = k * lax.rsqrt(jnp.sum(k * k, axis=-1, keepdims=True) + RMS_EPS)
        kn.append(k)
        beta_b = jnp.broadcast_to(beta_all[:, h:h + 1], (c, dh))
        kbeta.append(k * beta_b)
        vbeta.append(v * beta_b)
        gwide.append(jnp.broadcast_to(g_all[:, C_HEADS + h:C_HEADS + h + 1], (c, c)))

    gc = [_dot(lo_incl2, _split2_rows(gwide[h][:, :dh])) for h in heads]
    gdiff = [_dot(lo_incl2, _split2_rows(gwide[h] * lo_strict_f)) for h in heads]
    kf = [kn[h].astype(BF16) for h in heads]
    akk = [_dot_nt(kbeta[h].astype(BF16), kf[h]) for h in heads]
    aqk = [_dot_nt((qn[h] * scale).astype(BF16), kf[h]) for h in heads]
    eye_f = (r == q_).astype(F32)
    nlm, xb = [], []
    for h in heads:
        decay = jnp.where(lo_incl, jnp.exp(gdiff[h]), 0.0)
        nl = jnp.where(lo_strict, -(akk[h] * decay), 0.0)
        nlm.append(nl.astype(BF16))
        intra_ref[:, h * c:(h + 1) * c] = (aqk[h] * decay).astype(BF16)
        xb.append((jnp.where((r >> 1) == (q_ >> 1), nl, 0.0) + eye_f).astype(BF16))

    bs = 2
    while bs < c:
        sh = bs.bit_length() - 1
        join = (((r >> (sh + 1)) == (q_ >> (sh + 1))) & ((r >> sh) != (q_ >> sh))).astype(BF16)
        zb = [(_dot(xb[h], nlm[h] * join) + eye_f).astype(BF16) for h in heads]
        xb = [_dot(zb[h], xb[h]).astype(BF16) for h in heads]
        bs *= 2

    off_diag = (r != q_).astype(BF16)
    egc = [jnp.exp(gc[h]) for h in heads]
    rhs = [jnp.concatenate([vbeta[h], kbeta[h] * egc[h]], axis=1) for h in heads]
    sol = [rhs[h] + _dot(xb[h] * off_diag, rhs[h].astype(BF16)) for h in heads]
    for h in heads:
        sl = slice(h * dh, (h + 1) * dh)
        total = gc[h][c - 1:c, :]
        u_ref[:, sl] = sol[h][:, :dh]
        w_ref[:, sl] = sol[h][:, dh:].astype(BF16)
        qg_ref[:, sl] = (qn[h] * egc[h] * scale).astype(BF16)
        kd = kn[h] * jnp.exp(total - gc[h])
        kdt_ref[0, sl, :] = kd.T.astype(BF16)
        eg_ref[0, :, sl] = jnp.broadcast_to(jnp.exp(total), (V7X_SUBLANES, dh))


def gdn_prep(cbuf, gates, conv_w, alog_row, dtb_row, batch, chunk=GDN_C):
    t = cbuf.shape[0]
    s = t // batch
    nc = s // chunk
    hb = chunk // V7X_SUBLANES
    w3 = 3 * C_WIDTH
    kern = functools.partial(_gdn_prep_kernel, chunk=chunk)
    row = lambda w: pl.BlockSpec((chunk, w), lambda b, i: (b * nc + i, 0))
    return pl.pallas_call(
        kern,
        grid=(batch, nc),
        in_specs=[
            row(w3),
            pl.BlockSpec((V7X_SUBLANES, w3), lambda b, i: (jnp.maximum((b * nc + i) * hb - 1, 0), 0)),
            row(V7X_LANES),
            pl.BlockSpec(conv_w.shape, lambda b, i: (0, 0)),
            pl.BlockSpec((1, V7X_LANES), lambda b, i: (0, 0)),
            pl.BlockSpec((1, V7X_LANES), lambda b, i: (0, 0)),
        ],
        out_specs=[
            row(C_WIDTH), row(C_WIDTH), row(C_WIDTH), row(C_HEADS * chunk),
            pl.BlockSpec((1, C_WIDTH, chunk), lambda b, i: (b * nc + i, 0, 0)),
            pl.BlockSpec((1, V7X_SUBLANES, C_WIDTH), lambda b, i: (b * nc + i, 0, 0)),
        ],
        out_shape=[
            jax.ShapeDtypeStruct((t, C_WIDTH), F32),
            jax.ShapeDtypeStruct((t, C_WIDTH), BF16),
            jax.ShapeDtypeStruct((t, C_WIDTH), BF16),
            jax.ShapeDtypeStruct((t, C_HEADS * chunk), BF16),
            jax.ShapeDtypeStruct((t // chunk, C_WIDTH, chunk), BF16),
            jax.ShapeDtypeStruct((t // chunk, V7X_SUBLANES, C_WIDTH), F32),
        ],
        scratch_shapes=[pltpu.VMEM((chunk + V7X_SUBLANES, w3), F32)],
        compiler_params=_cparams(("parallel", "arbitrary")),
        name="gdn_prep",
    )(cbuf, cbuf, gates, conv_w, alog_row, dtb_row)


def _gdn_seq_kernel(u_ref, w_ref, qg_ref, intra_ref, kdt_ref, eg_ref, z_ref, gn_ref, o_ref, s_ref, *, chunk):
    @pl.when(pl.program_id(1) == 0)
    def _():
        s_ref[...] = jnp.zeros_like(s_ref)

    c = chunk
    dh = C_HEAD_DIM
    heads = range(C_HEADS)
    sls = [slice(h * dh, (h + 1) * dh) for h in heads]
    gn = gn_ref[...]
    state = [s_ref[h] for h in heads]
    sb = [state[h].astype(BF16) for h in heads]
    ws = [_dot(w_ref[:, sls[h]], sb[h]) for h in heads]
    qs = [_dot(qg_ref[:, sls[h]], sb[h]) for h in heads]
    vnb = [(u_ref[:, sls[h]] - ws[h]).astype(BF16) for h in heads]
    o = [qs[h] + _dot(intra_ref[:, h * c:(h + 1) * c], vnb[h]) for h in heads]
    for h in heads:
        s_ref[h] = state[h] * eg_ref[0, 0:1, sls[h]] + _dot(kdt_ref[0, sls[h], :], vnb[h])
    for h in heads:
        on = o[h] * lax.rsqrt(jnp.mean(o[h] * o[h], axis=-1, keepdims=True) + RMS_EPS) * gn
        o_ref[:, sls[h]] = (on * _silu(z_ref[:, sls[h]])).astype(BF16)


def gdn_seq(u, w, qg, intra, kdt, eg, cbuf, norm_g, batch, chunk=GDN_C):
    t = u.shape[0]
    nc = t // batch // chunk
    row = pl.BlockSpec((chunk, C_WIDTH), lambda b, i: (b * nc + i, 0))
    return pl.pallas_call(
        functools.partial(_gdn_seq_kernel, chunk=chunk),
        grid=(batch, nc),
        in_specs=[
            row, row, row,
            pl.BlockSpec((chunk, C_HEADS * chunk), lambda b, i: (b * nc + i, 0)),
            pl.BlockSpec((1, C_WIDTH, chunk), lambda b, i: (b * nc + i, 0, 0)),
            pl.BlockSpec((1, V7X_SUBLANES, C_WIDTH), lambda b, i: (b * nc + i, 0, 0)),
            pl.BlockSpec((chunk, C_WIDTH), lambda b, i: (b * nc + i, 3)),
            pl.BlockSpec((1, C_HEAD_DIM), lambda b, i: (0, 0)),
        ],
        out_specs=row,
        out_shape=jax.ShapeDtypeStruct((t, C_WIDTH), BF16),
        scratch_shapes=[pltpu.VMEM((C_HEADS, C_HEAD_DIM, C_HEAD_DIM), F32)],
        compiler_params=_cparams(("parallel", "arbitrary")),
        name="gdn_seq",
    )(u, w, qg, intra, kdt, eg, cbuf, norm_g.reshape(1, C_HEAD_DIM))


def _out_kernel(x_ref, oa_ref, ob_ref, oc_ref, w_ref, g_ref, b_ref, o_ref):
    mix = _dot(oa_ref[...], w_ref[0:A_WIDTH, :])
    mix = mix + _dot(ob_ref[...], w_ref[A_WIDTH:A_WIDTH + B_WIDTH, :])
    mix = mix + _dot(oc_ref[...], w_ref[A_WIDTH + B_WIDTH:, :])
    y = DEEPNORM_ALPHA * x_ref[...] + mix
    o_ref[...] = _layer_norm(y, g_ref[...], b_ref[...])


def out_ln(x, oa, ob, oc, w_out, g, b, tm=OUT_TM):
    t, d = x.shape
    tm = min(tm, t)
    row = lambda w: pl.BlockSpec((tm, w), lambda i: (i, 0))
    return pl.pallas_call(
        _out_kernel,
        grid=(t // tm,),
        in_specs=[row(d), row(A_WIDTH), row(B_WIDTH), row(C_WIDTH),
                  pl.BlockSpec(w_out.shape, lambda i: (0, 0)),
                  pl.BlockSpec((1, d), lambda i: (0, 0)), pl.BlockSpec((1, d), lambda i: (0, 0))],
        out_specs=row(d),
        out_shape=jax.ShapeDtypeStruct((t, d), F32),
        compiler_params=_cparams(("parallel",)),
        name="out_ln",
    )(x, oa, ob, oc, w_out, g, b)


def _split_w_in(w):
    o = 0
    parts = {}
    for name, n in (("qa", 256), ("ka", 256), ("va", 256), ("qb", 256), ("kb", 256), ("vb", 256),
                    ("qkv", 3 * C_WIDTH), ("z", C_WIDTH), ("beta", C_HEADS), ("a", C_HEADS)):
        parts[name] = w[:, o:o + n]
        o += n
    wk = jnp.concatenate([parts["ka"], parts["kb"]], axis=1).astype(BF16)
    wt = jnp.concatenate([parts["qa"], parts["va"], parts["qb"], parts["vb"]], axis=1).T.astype(BF16)
    wc = jnp.concatenate([parts["qkv"], parts["z"]], axis=1).astype(BF16)
    pad = jnp.zeros((w.shape[0], V7X_LANES - 2 * C_HEADS), w.dtype)
    wg = jnp.concatenate([parts["beta"], parts["a"], pad], axis=1).astype(BF16)
    return wk, wt, wc, wg


def _gate_row(vals):
    row = jnp.zeros((1, V7X_LANES), F32)
    return row.at[0, C_HEADS:2 * C_HEADS].set(vals.astype(F32))


def kernel(x, ffn1_w_gu, ffn1_w_down, ffn2_w_gu, ffn2_w_down, ln_g, ln_b, w_in, conv_w, dn_a_log, dn_dt_bias,
           dn_norm_g, diff_lambda, diff_norm_g, sb_norm_g, w_out):
    batch, seq, d = x.shape
    h = x.reshape(batch * seq, d)
    for l in range(DEPTH):
        lambda_init = 0.8 - 0.6 * math.exp(-0.3 * l)
        ln = lambda i: (ln_g[l, i].reshape(1, d), ln_b[l, i].reshape(1, d))
        h = ffn_ln(h, ffn1_w_gu[l].astype(BF16), ffn1_w_down[l].astype(BF16), *ln(0))
        wk, wt, wc, wg = _split_w_in(w_in[l])
        ka, kb, qta, vta, qtb, vtb, cbuf, gates = in_proj(h, wk, wt, wc, wg)
        oa = diff_attn(qta, ka, vta, diff_lambda[l], diff_norm_g[l], batch, lambda_init)
        ob = stick_attn(qtb, kb, vtb, sb_norm_g[l], batch)
        u, w, qg, intra, kdt, eg = gdn_prep(cbuf, gates, conv_w[l], _gate_row(dn_a_log[l]),
                                            _gate_row(dn_dt_bias[l]), batch)
        oc = gdn_seq(u, w, qg, intra, kdt, eg, cbuf, dn_norm_g[l], batch)
        h = out_ln(h, oa, ob, oc, w_out[l].astype(BF16), *ln(1))
        h = ffn_ln(h, ffn2_w_gu[l].astype(BF16), ffn2_w_down[l].astype(BF16), *ln(2))
    return h.reshape(batch, seq, d)
```

```python
import functools
import math

import jax
import jax.numpy as jnp
from jax import lax
from jax.experimental import pallas as pl
from jax.experimental.pallas import tpu as pltpu

F32 = jnp.float32
BF16 = jnp.bfloat16

DEPTH = 2
A_HEADS, A_QK_DIM, A_V_DIM = 4, 32, 64
B_HEADS, B_HEAD_DIM = 4, 64
C_HEADS, C_HEAD_DIM = 4, 128
CONV_K = 4
D_FF = 2816
A_WIDTH, B_WIDTH, C_WIDTH = 256, 256, 512
DEEPNORM_ALPHA = (2.0 * DEPTH) ** 0.25
LN_EPS = 1e-5
RMS_EPS = 1e-6
NEG_BIG = -1e30
LOG2_E = math.log2(math.e)
STICK_DEAD_LOG = 104.0

V7X_LANES = 128
V7X_SUBLANES = 8
V7X_VMEM_LIMIT_BYTES = 56 * 1024 * 1024
ONES_ROWS = 16

FFN_TM = 1024
FFN_TF = 256
PROJ_TM = 256
ATT_T = 256
GDN_C = 256
OUT_TM = 512


def _cparams(sem):
    return pltpu.CompilerParams(dimension_semantics=sem, vmem_limit_bytes=V7X_VMEM_LIMIT_BYTES)


def _layer_norm(y, g, b):
    mu = jnp.mean(y, axis=-1, keepdims=True)
    d = y - mu
    var = jnp.mean(d * d, axis=-1, keepdims=True)
    return d * lax.rsqrt(var + LN_EPS) * g + b


def _sigmoid(x):
    return 0.5 * jnp.tanh(0.5 * x) + 0.5


def _silu(x):
    return x * _sigmoid(x)


def _softplus(x):
    return jnp.maximum(x, 0.0) + jnp.log1p(jnp.exp(-jnp.abs(x)))


def _dot(a, b):
    return jnp.dot(a, b, preferred_element_type=F32)


def _dot_nt(a, b):
    return lax.dot_general(a, b, (((1,), (1,)), ((), ())), preferred_element_type=F32)


def _split2_rows(x):
    hi = x.astype(BF16)
    lo = (x - hi.astype(F32)).astype(BF16)
    return jnp.concatenate([hi, lo], axis=0)


def _ffn_kernel(x_ref, wg_ref, wu_ref, wd_ref, g_ref, b_ref, o_ref, xb_ref, acc_ref):
    j = pl.program_id(1)

    @pl.when(j == 0)
    def _():
        xb_ref[...] = x_ref[...].astype(BF16)
        acc_ref[...] = jnp.zeros_like(acc_ref)

    xb = xb_ref[...]
    gate = _dot(xb, wg_ref[...].astype(BF16))
    up = _dot(xb, wu_ref[...].astype(BF16))
    act = (_silu(gate) * up).astype(BF16)
    acc_ref[...] += _dot(act, wd_ref[...].astype(BF16))

    @pl.when(j == pl.num_programs(1) - 1)
    def _():
        y = DEEPNORM_ALPHA * x_ref[...] + 0.5 * acc_ref[...]
        o_ref[...] = _layer_norm(y, g_ref[...], b_ref[...])


def ffn_ln(x, w_gu, w_down, g, b, tm=FFN_TM, tf=FFN_TF):
    t, d = x.shape
    ff = w_down.shape[0]
    tm = min(tm, t)
    nf = ff // tf
    return pl.pallas_call(
        _ffn_kernel,
        grid=(t // tm, nf),
        in_specs=[
            pl.BlockSpec((tm, d), lambda i, j: (i, 0)),
            pl.BlockSpec((d, tf), lambda i, j: (0, j)),
            pl.BlockSpec((d, tf), lambda i, j: (0, j + nf)),
            pl.BlockSpec((tf, d), lambda i, j: (j, 0)),
            pl.BlockSpec((1, d), lambda i, j: (0, 0)),
            pl.BlockSpec((1, d), lambda i, j: (0, 0)),
        ],
        out_specs=pl.BlockSpec((tm, d), lambda i, j: (i, 0)),
        out_shape=jax.ShapeDtypeStruct((t, d), F32),
        scratch_shapes=[pltpu.VMEM((tm, d), BF16), pltpu.VMEM((tm, d), F32)],
        compiler_params=_cparams(("parallel", "arbitrary")),
        name="ffn_ln",
    )(x, w_gu, w_gu, w_down, g, b)


def _proj_kernel(x_ref, wk_ref, wt_ref, wc_ref, wg_ref,
                 ka_ref, kb_ref, qta_ref, vta_ref, qtb_ref, vtb_ref, c_ref, gate_ref):
    xb = x_ref[...].astype(BF16)
    k = _dot(xb, wk_ref[...])
    ka_ref[...] = k[:, :A_WIDTH].astype(BF16)
    kb_ref[...] = k[:, A_WIDTH:].astype(BF16)
    t = _dot_nt(wt_ref[...], xb)
    qta_ref[0] = (t[0:256] * (A_QK_DIM ** -0.5 * LOG2_E)).astype(BF16)
    vta_ref[0] = t[256:512].astype(BF16)
    qtb_ref[0] = (t[512:768] * (B_HEAD_DIM ** -0.5)).astype(BF16)
    vtb_ref[0] = t[768:1024].astype(BF16)
    c_ref[...] = _dot(xb, wc_ref[...])
    gate_ref[...] = _dot(xb, wg_ref[...])


def in_proj(x, wk, wt, wc, wg, tm=PROJ_TM):
    t, d = x.shape
    nt = t // tm
    full = lambda a: pl.BlockSpec(a.shape, lambda i: (0,) * a.ndim)
    row = lambda w: pl.BlockSpec((tm, w), lambda i: (i, 0))
    tr = pl.BlockSpec((1, 256, tm), lambda i: (i, 0, 0))
    tr_shape = jax.ShapeDtypeStruct((nt, 256, tm), BF16)
    return pl.pallas_call(
        _proj_kernel,
        grid=(nt,),
        in_specs=[row(d), full(wk), full(wt), full(wc), full(wg)],
        out_specs=[row(256), row(256), tr, tr, tr, tr, row(wc.shape[1]), row(V7X_LANES)],
        out_shape=[
            jax.ShapeDtypeStruct((t, 256), BF16), jax.ShapeDtypeStruct((t, 256), BF16),
            tr_shape, tr_shape, tr_shape, tr_shape,
            jax.ShapeDtypeStruct((t, wc.shape[1]), F32), jax.ShapeDtypeStruct((t, V7X_LANES), F32),
        ],
        compiler_params=_cparams(("parallel",)),
        name="in_proj",
    )(x, wk, wt, wc, wg)


def _diff_attn_kernel(q_ref, k_ref, v_ref, lam_ref, g_ref, o_ref,
                      km_ref, m_ref, a_ref, acc_ref, s_ref, tm_ref, p_ref, ot_ref, *, lambda_init, tile, nq):
    qi = pl.program_id(1)
    nc = 2 * A_HEADS
    qt = q_ref[0]

    @pl.when(qi == 0)
    def _():
        lane = lax.broadcasted_iota(jnp.int32, (tile, 256), 1)
        for j in range(nq):
            kj = k_ref[j * tile:(j + 1) * tile, :]
            for c in range(nc):
                keep = (lane >= c * A_QK_DIM) & (lane < (c + 1) * A_QK_DIM)
                km_ref[j, c * tile:(c + 1) * tile, :] = jnp.where(keep, kj, jnp.zeros_like(kj))

    def scores(j, slot):
        s = _dot(km_ref[j], qt)
        s_ref[slot] = s
        for c in range(nc):
            tm_ref[slot, c] = jnp.max(s[c * tile:(c + 1) * tile], axis=0, keepdims=True)

    m_ref[...] = jnp.full(m_ref.shape, NEG_BIG, F32)
    acc_ref[...] = jnp.zeros_like(acc_ref)
    ones = jnp.ones((ONES_ROWS, tile), BF16)

    def step(cur, src, nxt, dst, masked):
        vt = v_ref[cur]
        if masked:
            key = lax.broadcasted_iota(jnp.int32, (tile, tile), 0)
            qry = lax.broadcasted_iota(jnp.int32, (tile, tile), 1)
            causal = key <= qry
        if nxt is not None:
            scores(nxt, dst)
        m_new = []
        for c in range(nc):
            if masked:
                tmax = jnp.max(jnp.where(causal, s_ref[src, c * tile:(c + 1) * tile, :], NEG_BIG),
                               axis=0, keepdims=True)
            else:
                tmax = tm_ref[src, c]
            m_old = m_ref[c]
            m_new.append(jnp.maximum(m_old, tmax))
            a_ref[c] = jnp.exp2(m_old - m_new[c])
            m_ref[c] = m_new[c]

        def pv(c):
            h = c // 2
            vt_ext = jnp.concatenate([vt[h * A_V_DIM:(h + 1) * A_V_DIM, :], ones], axis=0)
            acc_ref[c] = a_ref[c] * acc_ref[c] + _dot(vt_ext, p_ref[c])

        for c in range(nc):
            s = s_ref[src, c * tile:(c + 1) * tile, :]
            if masked:
                s = jnp.where(causal, s, NEG_BIG)
            p_ref[c] = jnp.exp2((s - m_new[c]).astype(BF16))
            if c > 0:
                pv(c - 1)
        pv(nc - 1)

    scores(0, 0)

    def body(jj, carry):
        j = 2 * jj
        step(j, 0, j + 1, 1, False)
        step(j + 1, 1, j + 2, 0, False)
        return carry

    lax.fori_loop(0, qi // 2, body, 0)

    @pl.when(qi % 2 == 0)
    def _():
        step(qi, 0, None, None, True)

    @pl.when(qi % 2 == 1)
    def _():
        step(qi - 1, 0, qi, 1, False)
        step(qi, 1, None, None, True)

    lf = lam_ref[...]
    lam = (jnp.exp(jnp.sum(lf[0:1] * lf[1:2], axis=-1, keepdims=True))
           - jnp.exp(jnp.sum(lf[2:3] * lf[3:4], axis=-1, keepdims=True)) + lambda_init)
    g = g_ref[...]
    dv = A_V_DIM
    for h in range(A_HEADS):
        a1 = acc_ref[2 * h]
        a2 = acc_ref[2 * h + 1]
        o = a1[:dv] / a1[dv:dv + 1] - lam * (a2[:dv] / a2[dv:dv + 1])
        ms = jnp.mean(o * o, axis=0, keepdims=True)
        ot_ref[h * A_V_DIM:(h + 1) * A_V_DIM, :] = o * lax.rsqrt(ms + RMS_EPS) * g * (1.0 - lambda_init)
    o_ref[...] = ot_ref[...].T.astype(BF16)


def diff_attn(qt, k, vt, diff_lambda, norm_g, batch, lambda_init, tile=ATT_T):
    t = k.shape[0]
    s = t // batch
    nq = s // tile
    nc = 2 * A_HEADS
    kern = functools.partial(_diff_attn_kernel, lambda_init=lambda_init, tile=tile, nq=nq)
    return pl.pallas_call(
        kern,
        grid=(batch, nq),
        in_specs=[
            pl.BlockSpec((1, 256, tile), lambda b, i: (b * nq + i, 0, 0)),
            pl.BlockSpec((s, 256), lambda b, i: (b, 0)),
            pl.BlockSpec((nq, 256, tile), lambda b, i: (b, 0, 0)),
            pl.BlockSpec(diff_lambda.shape, lambda b, i: (0, 0)),
            pl.BlockSpec((A_V_DIM, 1), lambda b, i: (0, 0)),
        ],
        out_specs=pl.BlockSpec((tile, 256), lambda b, i: (b * nq + i, 0)),
        out_shape=jax.ShapeDtypeStruct((t, 256), BF16),
        scratch_shapes=[
            pltpu.VMEM((nq, nc * tile, 256), BF16),
            pltpu.VMEM((nc, 1, tile), F32),
            pltpu.VMEM((nc, 1, tile), F32),
            pltpu.VMEM((nc, A_V_DIM + ONES_ROWS, tile), F32),
            pltpu.VMEM((2, nc * tile, tile), F32),
            pltpu.VMEM((2, nc, 1, tile), F32),
            pltpu.VMEM((nc, tile, tile), BF16),
            pltpu.VMEM((256, tile), F32),
        ],
        compiler_params=_cparams(("parallel", "arbitrary")),
        name="diff_attn",
    )(qt, k, vt, diff_lambda, norm_g.reshape(A_V_DIM, 1))


def _stick_attn_kernel(q_ref, k_ref, v_ref, g_ref, o_ref,
                       qm_ref, carry_ref, acc_ref, z_ref, hl_ref, t_ref, w_ref, ot_ref, *, tile):
    qi = pl.program_id(1)
    qt = q_ref[0]
    row = lax.broadcasted_iota(jnp.int32, qt.shape, 0)
    for h in range(B_HEADS):
        keep = (row >= h * B_HEAD_DIM) & (row < (h + 1) * B_HEAD_DIM)
        qm_ref[h] = jnp.where(keep, qt, jnp.zeros_like(qt))
    carry_ref[...] = jnp.zeros_like(carry_ref)
    acc_ref[...] = jnp.zeros_like(acc_ref)

    key = lax.broadcasted_iota(jnp.int32, (tile, tile), 0)
    qry = lax.broadcasted_iota(jnp.int32, (tile, tile), 1)
    upper = (qry > key).astype(BF16)
    upper2 = jnp.concatenate([upper, upper], axis=1)
    strict = key < qry

    def do_tile(j, masked):
        kb = k_ref[pl.ds(pl.multiple_of(j * tile, tile), tile), :]
        vt = v_ref[j]
        for h in range(B_HEADS):
            z_ref[h] = _dot(kb, qm_ref[h])
        for h in range(B_HEADS):
            z = z_ref[h]
            sp = jnp.maximum(z, 0.0) + jnp.log(1.0 + jnp.exp(-jnp.abs(z)))
            if masked:
                sp = jnp.where(strict, sp, 0.0)
            hi = sp.astype(BF16)
            hl_ref[h, 0:tile, :] = hi
            hl_ref[h, tile:, :] = (sp - hi.astype(F32)).astype(BF16)
            carry = carry_ref[h]
            z_ref[h] = z - sp - carry
            carry_ref[h] = carry + jnp.sum(sp, axis=0, keepdims=True)
        for h in range(B_HEADS):
            t_ref[h] = _dot(upper2, hl_ref[h])
        for h in range(B_HEADS):
            w = jnp.exp(z_ref[h] - t_ref[h])
            if masked:
                w = jnp.where(strict, w, 0.0)
            w_ref[h] = w.astype(BF16)
        for h in range(B_HEADS):
            acc_ref[h] += _dot(vt[h * B_HEAD_DIM:(h + 1) * B_HEAD_DIM, :], w_ref[h])

    do_tile(qi, True)

    def live():
        return jnp.min(carry_ref[...]) < STICK_DEAD_LOG

    def cond(state):
        jj, go = state
        return jnp.logical_and(jj < qi, go)

    def body(state):
        jj, _ = state
        do_tile(qi - 1 - jj, False)
        return jj + 1, live()

    lax.while_loop(cond, body, (jnp.int32(0), live()))

    g = g_ref[...]
    for h in range(B_HEADS):
        o = acc_ref[h]
        ms = jnp.mean(o * o, axis=0, keepdims=True)
        ot_ref[h * B_HEAD_DIM:(h + 1) * B_HEAD_DIM, :] = o * lax.rsqrt(ms + RMS_EPS) * g
    o_ref[...] = ot_ref[...].T.astype(BF16)


def stick_attn(qt, k, vt, norm_g, batch, tile=ATT_T):
    t = k.shape[0]
    s = t // batch
    nq = s // tile
    kern = functools.partial(_stick_attn_kernel, tile=tile)
    return pl.pallas_call(
        kern,
        grid=(batch, nq),
        in_specs=[
            pl.BlockSpec((1, 256, tile), lambda b, i: (b * nq + i, 0, 0)),
            pl.BlockSpec((s, 256), lambda b, i: (b, 0)),
            pl.BlockSpec((nq, 256, tile), lambda b, i: (b, 0, 0)),
            pl.BlockSpec((B_HEAD_DIM, 1), lambda b, i: (0, 0)),
        ],
        out_specs=pl.BlockSpec((tile, 256), lambda b, i: (b * nq + i, 0)),
        out_shape=jax.ShapeDtypeStruct((t, 256), BF16),
        scratch_shapes=[
            pltpu.VMEM((B_HEADS, 256, tile), BF16),
            pltpu.VMEM((B_HEADS, 1, tile), F32),
            pltpu.VMEM((B_HEADS, B_HEAD_DIM, tile), F32),
            pltpu.VMEM((B_HEADS, tile, tile), F32),
            pltpu.VMEM((B_HEADS, 2 * tile, tile), BF16),
            pltpu.VMEM((B_HEADS, tile, tile), F32),
            pltpu.VMEM((B_HEADS, tile, tile), BF16),
            pltpu.VMEM((256, tile), F32),
        ],
        compiler_params=_cparams(("parallel", "arbitrary")),
        name="stick_attn",
    )(qt, k, vt, norm_g.reshape(B_HEAD_DIM, 1))


def _gdn_kernel(x_ref, halo_ref, gate_ref, cw_ref, alog_ref, dtb_ref, z_ref, gn_ref,
                o_ref, xs_ref, s_ref, *, chunk):
    ci = pl.program_id(1)

    @pl.when(ci == 0)
    def _():
        s_ref[...] = jnp.zeros_like(s_ref)

    c = chunk
    dh = C_HEAD_DIM
    heads = range(C_HEADS)
    halo = jnp.where(ci > 0, halo_ref[...], 0.0)
    xs_ref[0:V7X_SUBLANES, :] = halo
    xs_ref[V7X_SUBLANES:, :] = x_ref[...]
    cw = cw_ref[...]
    conv = cw[0:1] * xs_ref[pl.ds(V7X_SUBLANES - 3, c), :]
    for i in range(1, CONV_K):
        conv = conv + cw[i:i + 1] * xs_ref[pl.ds(V7X_SUBLANES - 3 + i, c), :]
    y = _silu(conv)

    gates = gate_ref[...]
    beta_all = _sigmoid(gates)
    g_all = -jnp.exp(alog_ref[...]) * _softplus(gates + dtb_ref[...])

    r = lax.broadcasted_iota(jnp.int32, (c, c), 0)
    q_ = lax.broadcasted_iota(jnp.int32, (c, c), 1)
    lo_incl = q_ <= r
    lo_strict = q_ < r
    lo_incl_b = lo_incl.astype(BF16)
    lo_incl2 = jnp.concatenate([lo_incl_b, lo_incl_b], axis=1)
    lo_strict_f = lo_strict.astype(F32)
    scale = dh ** -0.5

    qn, kn, kbeta, vbeta, gwide = [], [], [], [], []
    for h in heads:
        q = y[:, h * dh:(h + 1) * dh]
        k = y[:, C_WIDTH + h * dh:C_WIDTH + (h + 1) * dh]
        v = y[:, 2 * C_WIDTH + h * dh:2 * C_WIDTH + (h + 1) * dh]
        qn.append(q * lax.rsqrt(jnp.sum(q * q, axis=-1, keepdims=True) + RMS_EPS))
        k = k * lax.rsqrt(jnp.sum(k * k, axis=-1, keepdims=True) + RMS_EPS)
        kn.append(k)
        beta_b = jnp.broadcast_to(beta_all[:, h:h + 1], (c, dh))
        kbeta.append(k * beta_b)
        vbeta.append(v * beta_b)
        gwide.append(jnp.broadcast_to(g_all[:, C_HEADS + h:C_HEADS + h + 1], (c, c)))

    gc = [_dot(lo_incl2, _split2_rows(gwide[h][:, :dh])) for h in heads]
    gdiff = [_dot(lo_incl2, _split2_rows(gwide[h] * lo_strict_f)) for h in heads]
    kf = [kn[h].astype(BF16) for h in heads]
    akk = [_dot_nt(kbeta[h].astype(BF16), kf[h]) for h in heads]
    aqk = [_dot_nt((qn[h] * scale).astype(BF16), kf[h]) for h in heads]
    eye_f = (r == q_).astype(F32)
    nlm, xb, intra = [], [], []
    for h in heads:
        decay = jnp.where(lo_incl, jnp.exp(gdiff[h]), 0.0)
        nl = jnp.where(lo_strict, -(akk[h] * decay), 0.0)
        nlm.append(nl.astype(BF16))
        intra.append((aqk[h] * decay).astype(BF16))
        xb.append((jnp.where((r >> 1) == (q_ >> 1), nl, 0.0) + eye_f).astype(BF16))

    bs = 2
    while bs < c:
        sh = bs.bit_length() - 1
        join = (((r >> (sh + 1)) == (q_ >> (sh + 1))) & ((r >> sh) != (q_ >> sh))).astype(BF16)
        zb = [(_dot(xb[h], nlm[h] * join) + eye_f).astype(BF16) for h in heads]
        xb = [_dot(zb[h], xb[h]).astype(BF16) for h in heads]
        bs *= 2

    off_diag = (r != q_).astype(BF16)
    egc = [jnp.exp(gc[h]) for h in heads]
    rhs = [jnp.concatenate([vbeta[h], kbeta[h] * egc[h]], axis=1) for h in heads]
    sol = [rhs[h] + _dot(xb[h] * off_diag, rhs[h].astype(BF16)) for h in heads]
    state = [s_ref[h] for h in heads]
    sb = [state[h].astype(BF16) for h in heads]
    ws = [_dot(sol[h][:, dh:].astype(BF16), sb[h]) for h in heads]
    qs = [_dot((qn[h] * egc[h] * scale).astype(BF16), sb[h]) for h in heads]
    vnb = [(sol[h][:, :dh] - ws[h]).astype(BF16) for h in heads]
    o = [qs[h] + _dot(intra[h], vnb[h]) for h in heads]
    gn = gn_ref[...]
    for h in heads:
        total = gc[h][c - 1:c, :]
        kdt = (kn[h] * jnp.exp(total - gc[h])).T.astype(BF16)
        s_ref[h] = state[h] * jnp.exp(total) + _dot(kdt, vnb[h])
    for h in heads:
        sl = slice(h * dh, (h + 1) * dh)
        on = o[h] * lax.rsqrt(jnp.mean(o[h] * o[h], axis=-1, keepdims=True) + RMS_EPS) * gn
        o_ref[:, sl] = (on * _silu(z_ref[:, sl])).astype(BF16)


def gdn(cbuf, gates, conv_w, alog_row, dtb_row, norm_g, batch, chunk=GDN_C):
    t = cbuf.shape[0]
    s = t // batch
    nc = s // chunk
    hb = chunk // V7X_SUBLANES
    w3 = 3 * C_WIDTH
    kern = functools.partial(_gdn_kernel, chunk=chunk)
    row = lambda w: pl.BlockSpec((chunk, w), lambda b, i: (b * nc + i, 0))
    return pl.pallas_call(
        kern,
        grid=(batch, nc),
        in_specs=[
            row(w3),
            pl.BlockSpec((V7X_SUBLANES, w3), lambda b, i: (jnp.maximum((b * nc + i) * hb - 1, 0), 0)),
            row(V7X_LANES),
            pl.BlockSpec(conv_w.shape, lambda b, i: (0, 0)),
            pl.BlockSpec((1, V7X_LANES), lambda b, i: (0, 0)),
            pl.BlockSpec((1, V7X_LANES), lambda b, i: (0, 0)),
            pl.BlockSpec((chunk, C_WIDTH), lambda b, i: (b * nc + i, 3)),
            pl.BlockSpec((1, C_HEAD_DIM), lambda b, i: (0, 0)),
        ],
        out_specs=row(C_WIDTH),
        out_shape=jax.ShapeDtypeStruct((t, C_WIDTH), BF16),
        scratch_shapes=[pltpu.VMEM((chunk + V7X_SUBLANES, w3), F32),
                        pltpu.VMEM((C_HEADS, C_HEAD_DIM, C_HEAD_DIM), F32)],
        compiler_params=_cparams(("parallel", "arbitrary")),
        name="gdn",
    )(cbuf, cbuf, gates, conv_w, alog_row, dtb_row, cbuf, norm_g.reshape(1, C_HEAD_DIM))


def _out_kernel(x_ref, oa_ref, ob_ref, oc_ref, w_ref, g_ref, b_ref, o_ref):
    mix = _dot(oa_ref[...], w_ref[0:A_WIDTH, :])
    mix = mix + _dot(ob_ref[...], w_ref[A_WIDTH:A_WIDTH + B_WIDTH, :])
    mix = mix + _dot(oc_ref[...], w_ref[A_WIDTH + B_WIDTH:, :])
    y = DEEPNORM_ALPHA * x_ref[...] + mix
    o_ref[...] = _layer_norm(y, g_ref[...], b_ref[...])


def out_ln(x, oa, ob, oc, w_out, g, b, tm=OUT_TM):
    t, d = x.shape
    tm = min(tm, t)
    row = lambda w: pl.BlockSpec((tm, w), lambda i: (i, 0))
    return pl.pallas_call(
        _out_kernel,
        grid=(t // tm,),
        in_specs=[row(d), row(A_WIDTH), row(B_WIDTH), row(C_WIDTH),
                  pl.BlockSpec(w_out.shape, lambda i: (0, 0)),
                  pl.BlockSpec((1, d), lambda i: (0, 0)), pl.BlockSpec((1, d), lambda i: (0, 0))],
        out_specs=row(d),
        out_shape=jax.ShapeDtypeStruct((t, d), F32),
        compiler_params=_cparams(("parallel",)),
        name="out_ln",
    )(x, oa, ob, oc, w_out, g, b)


def _split_w_in(w):
    o = 0
    parts = {}
    for name, n in (("qa", 256), ("ka", 256), ("va", 256), ("qb", 256), ("kb", 256), ("vb", 256),
                    ("qkv", 3 * C_WIDTH), ("z", C_WIDTH), ("beta", C_HEADS), ("a", C_HEADS)):
        parts[name] = w[:, o:o + n]
        o += n
    wk = jnp.concatenate([parts["ka"], parts["kb"]], axis=1).astype(BF16)
    wt = jnp.concatenate([parts["qa"], parts["va"], parts["qb"], parts["vb"]], axis=1).T.astype(BF16)
    wc = jnp.concatenate([parts["qkv"], parts["z"]], axis=1).astype(BF16)
    pad = jnp.zeros((w.shape[0], V7X_LANES - 2 * C_HEADS), w.dtype)
    wg = jnp.concatenate([parts["beta"], parts["a"], pad], axis=1).astype(BF16)
    return wk, wt, wc, wg


def _gate_row(vals):
    row = jnp.zeros((1, V7X_LANES), F32)
    return row.at[0, C_HEADS:2 * C_HEADS].set(vals.astype(F32))


def kernel(x, ffn1_w_gu, ffn1_w_down, ffn2_w_gu, ffn2_w_down, ln_g, ln_b, w_in, conv_w, dn_a_log, dn_dt_bias,
           dn_norm_g, diff_lambda, diff_norm_g, sb_norm_g, w_out):
    batch, seq, d = x.shape
    h = x.reshape(batch * seq, d)
    for l in range(DEPTH):
        lambda_init = 0.8 - 0.6 * math.exp(-0.3 * l)
        ln = lambda i: (ln_g[l, i].reshape(1, d), ln_b[l, i].reshape(1, d))
        h = ffn_ln(h, ffn1_w_gu[l], ffn1_w_down[l], *ln(0))
        wk, wt, wc, wg = _split_w_in(w_in[l])
        ka, kb, qta, vta, qtb, vtb, cbuf, gates = in_proj(h, wk, wt, wc, wg)
        oa = diff_attn(qta, ka, vta, diff_lambda[l], diff_norm_g[l], batch, lambda_init)
        ob = stick_attn(qtb, kb, vtb, sb_norm_g[l], batch)
        oc = gdn(cbuf, gates, conv_w[l], _gate_row(dn_a_log[l]), _gate_row(dn_dt_bias[l]), dn_norm_g[l], batch)
        h = out_ln(h, oa, ob, oc, w_out[l].astype(BF16), *ln(1))
        h = ffn_ln(h, ffn2_w_gu[l], ffn2_w_down[l], *ln(2))
    return h.reshape(batch, seq, d)
```

```python
import functools
import math

import jax
import jax.numpy as jnp
import numpy as np
from jax import lax
from jax.experimental import pallas as pl
from jax.experimental.pallas import tpu as pltpu

F32 = jnp.float32
BF16 = jnp.bfloat16

DEPTH = 2
A_HEADS, A_QK_DIM, A_V_DIM = 4, 32, 64
B_HEADS, B_HEAD_DIM = 4, 64
C_HEADS, C_HEAD_DIM = 4, 128
CONV_K = 4
D_FF = 2816
A_WIDTH, B_WIDTH, C_WIDTH = 256, 256, 512
DEEPNORM_ALPHA = (2.0 * DEPTH) ** 0.25
LN_EPS = 1e-5
RMS_EPS = 1e-6
NEG_BIG = -1e30
LOG2_E = math.log2(math.e)
STICK_DEAD_LOG = 104.0

V7X_LANES = 128
V7X_SUBLANES = 8
V7X_VMEM_LIMIT_BYTES = 56 * 1024 * 1024
ONES_ROWS = 16

FFN_TM = 1024
FFN_TF = 256
PROJ_TM = 256
ATT_T = 256
GDN_C = 256
OUT_TM = 512
CONV_COLS = 512


def _cparams(sem):
    return pltpu.CompilerParams(dimension_semantics=sem, vmem_limit_bytes=V7X_VMEM_LIMIT_BYTES)


def _layer_norm(y, g, b):
    mu = jnp.mean(y, axis=-1, keepdims=True)
    d = y - mu
    var = jnp.mean(d * d, axis=-1, keepdims=True)
    return d * lax.rsqrt(var + LN_EPS) * g + b


def _sigmoid(x):
    return 0.5 * jnp.tanh(0.5 * x) + 0.5


def _silu(x):
    return x * _sigmoid(x)


def _softplus(x):
    return jnp.maximum(x, 0.0) + jnp.log1p(jnp.exp(-jnp.abs(x)))


def _dot(a, b):
    return jnp.dot(a, b, preferred_element_type=F32)


def _dot_nt(a, b):
    return lax.dot_general(a, b, (((1,), (1,)), ((), ())), preferred_element_type=F32)


def _split2_rows(x):
    hi = x.astype(BF16)
    lo = (x - hi.astype(F32)).astype(BF16)
    return jnp.concatenate([hi, lo], axis=0)


def _ffn_kernel(x_ref, wg_ref, wu_ref, wd_ref, g_ref, b_ref, o_ref, xb_ref, acc_ref):
    j = pl.program_id(1)

    @pl.when(j == 0)
    def _():
        xb_ref[...] = x_ref[...].astype(BF16)
        acc_ref[...] = jnp.zeros_like(acc_ref)

    xb = xb_ref[...]
    gate = _dot(xb, wg_ref[...].astype(BF16))
    up = _dot(xb, wu_ref[...].astype(BF16))
    act = (_silu(gate) * up).astype(BF16)
    acc_ref[...] += _dot(act, wd_ref[...].astype(BF16))

    @pl.when(j == pl.num_programs(1) - 1)
    def _():
        y = DEEPNORM_ALPHA * x_ref[...] + 0.5 * acc_ref[...]
        o_ref[...] = _layer_norm(y, g_ref[...], b_ref[...])


def ffn_ln(x, w_gu, w_down, layer, g, b, tm=FFN_TM, tf=FFN_TF):
    t, d = x.shape
    ff = w_down.shape[1]
    tm = min(tm, t)
    nf = ff // tf
    return pl.pallas_call(
        _ffn_kernel,
        grid=(t // tm, nf),
        in_specs=[
            pl.BlockSpec((tm, d), lambda i, j: (i, 0)),
            pl.BlockSpec((None, d, tf), lambda i, j: (layer, 0, j)),
            pl.BlockSpec((None, d, tf), lambda i, j: (layer, 0, j + nf)),
            pl.BlockSpec((None, tf, d), lambda i, j: (layer, j, 0)),
            pl.BlockSpec((1, d), lambda i, j: (0, 0)),
            pl.BlockSpec((1, d), lambda i, j: (0, 0)),
        ],
        out_specs=pl.BlockSpec((tm, d), lambda i, j: (i, 0)),
        out_shape=jax.ShapeDtypeStruct((t, d), F32),
        scratch_shapes=[pltpu.VMEM((tm, d), BF16), pltpu.VMEM((tm, d), F32)],
        compiler_params=_cparams(("parallel", "arbitrary")),
        name="ffn_ln",
    )(x, w_gu, w_gu, w_down, g, b)


def _proj_kernel(x_ref, wk_ref, wt_ref, wc_ref, wg_ref,
                 ka_ref, kb_ref, qta_ref, vta_ref, qtb_ref, vtb_ref, c_ref, gate_ref):
    xb = x_ref[...].astype(BF16)
    c_ref[...] = _dot(xb, wc_ref[...])
    k = _dot(xb, wk_ref[...])
    ka_ref[...] = k[:, :A_WIDTH].astype(BF16)
    kb_ref[...] = k[:, A_WIDTH:].astype(BF16)
    t = _dot_nt(wt_ref[...], xb)
    qta_ref[0] = (t[0:256] * (A_QK_DIM ** -0.5 * LOG2_E)).astype(BF16)
    vta_ref[0] = t[256:512].astype(BF16)
    qtb_ref[0] = (t[512:768] * (B_HEAD_DIM ** -0.5)).astype(BF16)
    vtb_ref[0] = t[768:1024].astype(BF16)
    gate_ref[...] = _dot(xb, wg_ref[...])


def in_proj(x, wk, wt, wc, wg, tm=PROJ_TM):
    t, d = x.shape
    nt = t // tm
    full = lambda a: pl.BlockSpec(a.shape, lambda i: (0,) * a.ndim)
    row = lambda w: pl.BlockSpec((tm, w), lambda i: (i, 0))
    tr = pl.BlockSpec((1, 256, tm), lambda i: (i, 0, 0))
    tr_shape = jax.ShapeDtypeStruct((nt, 256, tm), BF16)
    return pl.pallas_call(
        _proj_kernel,
        grid=(nt,),
        in_specs=[row(d), full(wk), full(wt), full(wc), full(wg)],
        out_specs=[row(256), row(256), tr, tr, tr, tr, row(wc.shape[1]), row(V7X_LANES)],
        out_shape=[
            jax.ShapeDtypeStruct((t, 256), BF16), jax.ShapeDtypeStruct((t, 256), BF16),
            tr_shape, tr_shape, tr_shape, tr_shape,
            jax.ShapeDtypeStruct((t, wc.shape[1]), F32), jax.ShapeDtypeStruct((t, V7X_LANES), F32),
        ],
        compiler_params=_cparams(("parallel",)),
        name="in_proj",
    )(x, wk, wt, wc, wg)


def _diff_attn_kernel(q_ref, k_ref, v_ref, lam_ref, g_ref, o_ref,
                      km_ref, m_ref, a_ref, acc_ref, s_ref, tm_ref, p_ref, ot_ref, *, lambda_init, tile, nq):
    qi = pl.program_id(1)
    nc = 2 * A_HEADS
    qt = q_ref[0]

    @pl.when(qi == 0)
    def _():
        lane = lax.broadcasted_iota(jnp.int32, (tile, 256), 1)
        for j in range(nq):
            kj = k_ref[j * tile:(j + 1) * tile, :]
            for c in range(nc):
                keep = (lane >= c * A_QK_DIM) & (lane < (c + 1) * A_QK_DIM)
                km_ref[j, c * tile:(c + 1) * tile, :] = jnp.where(keep, kj, jnp.zeros_like(kj))

    def scores(j, slot):
        s = _dot(km_ref[j], qt)
        s_ref[slot] = s
        for c in range(nc):
            tm_ref[slot, c] = jnp.max(s[c * tile:(c + 1) * tile], axis=0, keepdims=True)

    m_ref[...] = jnp.full(m_ref.shape, NEG_BIG, F32)
    acc_ref[...] = jnp.zeros_like(acc_ref)
    ones = jnp.ones((ONES_ROWS, tile), BF16)

    def step(cur, src, nxt, dst, masked):
        vt = v_ref[cur]
        if masked:
            key = lax.broadcasted_iota(jnp.int32, (tile, tile), 0)
            qry = lax.broadcasted_iota(jnp.int32, (tile, tile), 1)
            causal = key <= qry
        if nxt is not None:
            scores(nxt, dst)
        m_new = []
        for c in range(nc):
            if masked:
                tmax = jnp.max(jnp.where(causal, s_ref[src, c * tile:(c + 1) * tile, :], NEG_BIG),
                               axis=0, keepdims=True)
            else:
                tmax = tm_ref[src, c]
            m_old = m_ref[c]
            m_new.append(jnp.maximum(m_old, tmax))
            a_ref[c] = jnp.exp2(m_old - m_new[c])
            m_ref[c] = m_new[c]

        def pv(c):
            h = c // 2
            vt_ext = jnp.concatenate([vt[h * A_V_DIM:(h + 1) * A_V_DIM, :], ones], axis=0)
            acc_ref[c] = a_ref[c] * acc_ref[c] + _dot(vt_ext, p_ref[c])

        for c in range(nc):
            s = s_ref[src, c * tile:(c + 1) * tile, :]
            if masked:
                s = jnp.where(causal, s, NEG_BIG)
            p_ref[c] = jnp.exp2((s - m_new[c]).astype(BF16))
            if c > 0:
                pv(c - 1)
        pv(nc - 1)

    scores(0, 0)

    def body(jj, carry):
        j = 2 * jj
        step(j, 0, j + 1, 1, False)
        step(j + 1, 1, j + 2, 0, False)
        return carry

    lax.fori_loop(0, qi // 2, body, 0)

    @pl.when(qi % 2 == 0)
    def _():
        step(qi, 0, None, None, True)

    @pl.when(qi % 2 == 1)
    def _():
        step(qi - 1, 0, qi, 1, False)
        step(qi, 1, None, None, True)

    lf = lam_ref[...]
    lam = (jnp.exp(jnp.sum(lf[0:1] * lf[1:2], axis=-1, keepdims=True))
           - jnp.exp(jnp.sum(lf[2:3] * lf[3:4], axis=-1, keepdims=True)) + lambda_init)
    g = g_ref[...]
    dv = A_V_DIM
    for h in range(A_HEADS):
        a1 = acc_ref[2 * h]
        a2 = acc_ref[2 * h + 1]
        o = a1[:dv] / a1[dv:dv + 1] - lam * (a2[:dv] / a2[dv:dv + 1])
        ms = jnp.mean(o * o, axis=0, keepdims=True)
        ot_ref[h * A_V_DIM:(h + 1) * A_V_DIM, :] = o * lax.rsqrt(ms + RMS_EPS) * g * (1.0 - lambda_init)
    o_ref[...] = ot_ref[...].T.astype(BF16)


def diff_attn(qt, k, vt, diff_lambda, norm_g, batch, lambda_init, tile=ATT_T):
    t = k.shape[0]
    s = t // batch
    nq = s // tile
    nc = 2 * A_HEADS
    kern = functools.partial(_diff_attn_kernel, lambda_init=lambda_init, tile=tile, nq=nq)
    return pl.pallas_call(
        kern,
        grid=(batch, nq),
        in_specs=[
            pl.BlockSpec((1, 256, tile), lambda b, i: (b * nq + i, 0, 0)),
            pl.BlockSpec((s, 256), lambda b, i: (b, 0)),
            pl.BlockSpec((nq, 256, tile), lambda b, i: (b, 0, 0)),
            pl.BlockSpec(diff_lambda.shape, lambda b, i: (0, 0)),
            pl.BlockSpec((A_V_DIM, 1), lambda b, i: (0, 0)),
        ],
        out_specs=pl.BlockSpec((tile, 256), lambda b, i: (b * nq + i, 0)),
        out_shape=jax.ShapeDtypeStruct((t, 256), BF16),
        scratch_shapes=[
            pltpu.VMEM((nq, nc * tile, 256), BF16),
            pltpu.VMEM((nc, 1, tile), F32),
            pltpu.VMEM((nc, 1, tile), F32),
            pltpu.VMEM((nc, A_V_DIM + ONES_ROWS, tile), F32),
            pltpu.VMEM((2, nc * tile, tile), F32),
            pltpu.VMEM((2, nc, 1, tile), F32),
            pltpu.VMEM((nc, tile, tile), BF16),
            pltpu.VMEM((256, tile), F32),
        ],
        compiler_params=_cparams(("parallel", "arbitrary")),
        name="diff_attn",
    )(qt, k, vt, diff_lambda, norm_g.reshape(A_V_DIM, 1))


def _stick_attn_kernel(q_ref, k_ref, v_ref, g_ref, o_ref,
                       qm_ref, carry_ref, acc_ref, z_ref, hl_ref, t_ref, w_ref, ot_ref, *, tile):
    qi = pl.program_id(1)
    qt = q_ref[0]
    row = lax.broadcasted_iota(jnp.int32, qt.shape, 0)
    for h in range(B_HEADS):
        keep = (row >= h * B_HEAD_DIM) & (row < (h + 1) * B_HEAD_DIM)
        qm_ref[h] = jnp.where(keep, qt, jnp.zeros_like(qt))
    carry_ref[...] = jnp.zeros_like(carry_ref)
    acc_ref[...] = jnp.zeros_like(acc_ref)

    key = lax.broadcasted_iota(jnp.int32, (tile, tile), 0)
    qry = lax.broadcasted_iota(jnp.int32, (tile, tile), 1)
    upper = (qry > key).astype(BF16)
    upper2 = jnp.concatenate([upper, upper], axis=1)
    strict = key < qry

    def do_tile(j, masked):
        kb = k_ref[pl.ds(pl.multiple_of(j * tile, tile), tile), :]
        vt = v_ref[j]
        for h in range(B_HEADS):
            z_ref[h] = _dot(kb, qm_ref[h])
        for h in range(B_HEADS):
            z = z_ref[h]
            sp = jnp.maximum(z, 0.0) + jnp.log(1.0 + jnp.exp(-jnp.abs(z)))
            if masked:
                sp = jnp.where(strict, sp, 0.0)
            hi = sp.astype(BF16)
            hl_ref[h, 0:tile, :] = hi
            hl_ref[h, tile:, :] = (sp - hi.astype(F32)).astype(BF16)
            carry = carry_ref[h]
            z_ref[h] = z - sp - carry
            carry_ref[h] = carry + jnp.sum(sp, axis=0, keepdims=True)
        for h in range(B_HEADS):
            t_ref[h] = _dot(upper2, hl_ref[h])
        for h in range(B_HEADS):
            w = jnp.exp(z_ref[h] - t_ref[h])
            if masked:
                w = jnp.where(strict, w, 0.0)
            w_ref[h] = w.astype(BF16)
        for h in range(B_HEADS):
            acc_ref[h] += _dot(vt[h * B_HEAD_DIM:(h + 1) * B_HEAD_DIM, :], w_ref[h])

    do_tile(qi, True)

    def live():
        return jnp.min(carry_ref[...]) < STICK_DEAD_LOG

    def cond(state):
        jj, go = state
        return jnp.logical_and(jj < qi, go)

    def body(state):
        jj, _ = state
        do_tile(qi - 1 - jj, False)
        return jj + 1, live()

    lax.while_loop(cond, body, (jnp.int32(0), live()))

    g = g_ref[...]
    for h in range(B_HEADS):
        o = acc_ref[h]
        ms = jnp.mean(o * o, axis=0, keepdims=True)
        ot_ref[h * B_HEAD_DIM:(h + 1) * B_HEAD_DIM, :] = o * lax.rsqrt(ms + RMS_EPS) * g
    o_ref[...] = ot_ref[...].T.astype(BF16)


def stick_attn(qt, k, vt, norm_g, batch, tile=ATT_T):
    t = k.shape[0]
    s = t // batch
    nq = s // tile
    kern = functools.partial(_stick_attn_kernel, tile=tile)
    return pl.pallas_call(
        kern,
        grid=(batch, nq),
        in_specs=[
            pl.BlockSpec((1, 256, tile), lambda b, i: (b * nq + i, 0, 0)),
            pl.BlockSpec((s, 256), lambda b, i: (b, 0)),
            pl.BlockSpec((nq, 256, tile), lambda b, i: (b, 0, 0)),
            pl.BlockSpec((B_HEAD_DIM, 1), lambda b, i: (0, 0)),
        ],
        out_specs=pl.BlockSpec((tile, 256), lambda b, i: (b * nq + i, 0)),
        out_shape=jax.ShapeDtypeStruct((t, 256), BF16),
        scratch_shapes=[
            pltpu.VMEM((B_HEADS, 256, tile), BF16),
            pltpu.VMEM((B_HEADS, 1, tile), F32),
            pltpu.VMEM((B_HEADS, B_HEAD_DIM, tile), F32),
            pltpu.VMEM((B_HEADS, tile, tile), F32),
            pltpu.VMEM((B_HEADS, 2 * tile, tile), BF16),
            pltpu.VMEM((B_HEADS, tile, tile), F32),
            pltpu.VMEM((B_HEADS, tile, tile), BF16),
            pltpu.VMEM((256, tile), F32),
        ],
        compiler_params=_cparams(("parallel", "arbitrary")),
        name="stick_attn",
    )(qt, k, vt, norm_g.reshape(B_HEAD_DIM, 1))


MF_STRICT, MF_NSTRICT, MF_INCL, MF_EYE, MF_PAIR = range(5)


def _gdn_masks(c):
    r = np.arange(c)[:, None]
    q = np.arange(c)[None, :]
    strict = (q < r).astype(np.float32)
    incl = (q <= r).astype(np.float32)
    eye = (q == r).astype(np.float32)
    pair = ((r >> 1) == (q >> 1)).astype(np.float32)
    mf = np.stack([strict, -strict, incl, eye, pair])
    joins = []
    bs = 2
    while bs < c:
        sh = bs.bit_length() - 1
        joins.append((((r >> (sh + 1)) == (q >> (sh + 1))) & ((r >> sh) != (q >> sh))).astype(np.float32))
        bs *= 2
    joins.append(1.0 - eye)
    return (jnp.asarray(mf), jnp.asarray(np.concatenate([incl, incl], axis=1), dtype=BF16),
            jnp.asarray(np.stack(joins), dtype=BF16))


def _gdn_kernel(x_ref, halo_ref, cw_ref, gate_ref, alog_ref, dtb_ref, z_ref, gn_ref, mf_ref, incl2_ref, mb_ref,
                o_ref, xs_ref, y_ref, s_ref, *, chunk):
    ci = pl.program_id(1)

    @pl.when(ci == 0)
    def _():
        s_ref[...] = jnp.zeros_like(s_ref)

    c = chunk
    dh = C_HEAD_DIM
    heads = range(C_HEADS)
    n_levels = mb_ref.shape[0] - 1

    xs_ref[0:V7X_SUBLANES, :] = jnp.where(ci > 0, halo_ref[...], 0.0)
    xs_ref[V7X_SUBLANES:, :] = x_ref[...]
    first = V7X_SUBLANES - CONV_K + 1
    for c0 in range(0, 3 * C_WIDTH, CONV_COLS):
        cols = slice(c0, c0 + CONV_COLS)
        xs = xs_ref[:, cols]
        conv = cw_ref[CONV_K - 1:CONV_K, cols] * xs[V7X_SUBLANES:]
        for back in range(1, CONV_K):
            shifted = pltpu.roll(xs, back, axis=0)[V7X_SUBLANES:]
            conv = conv + cw_ref[CONV_K - 1 - back:CONV_K - back, cols] * shifted
        y_ref[:, cols] = _silu(conv)

    gates = gate_ref[...]
    beta_all = _sigmoid(gates)
    g_all = -jnp.exp(alog_ref[...]) * _softplus(gates + dtb_ref[...])
    scale = dh ** -0.5

    qn, kn, kbeta, vbeta, gwide = [], [], [], [], []
    for h in heads:
        q = y_ref[:, h * dh:(h + 1) * dh]
        k = y_ref[:, C_WIDTH + h * dh:C_WIDTH + (h + 1) * dh]
        v = y_ref[:, 2 * C_WIDTH + h * dh:2 * C_WIDTH + (h + 1) * dh]
        qn.append(q * lax.rsqrt(jnp.sum(q * q, axis=-1, keepdims=True) + RMS_EPS))
        k = k * lax.rsqrt(jnp.sum(k * k, axis=-1, keepdims=True) + RMS_EPS)
        kn.append(k)
        beta_b = jnp.broadcast_to(beta_all[:, h:h + 1], (c, dh))
        kbeta.append(k * beta_b)
        vbeta.append(v * beta_b)
        gwide.append(jnp.broadcast_to(g_all[:, C_HEADS + h:C_HEADS + h + 1], (c, c)))

    gc = [_dot(incl2_ref[...], _split2_rows(gwide[h][:, :dh])) for h in heads]
    gdiff = [_dot(incl2_ref[...], _split2_rows(gwide[h] * mf_ref[MF_STRICT])) for h in heads]
    kf = [kn[h].astype(BF16) for h in heads]
    akk = [_dot_nt(kbeta[h].astype(BF16), kf[h]) for h in heads]
    aqk = [_dot_nt((qn[h] * scale).astype(BF16), kf[h]) for h in heads]
    nlm, xb, intra = [], [], []
    for h in heads:
        e = jnp.exp(gdiff[h])
        nl = akk[h] * (e * mf_ref[MF_NSTRICT])
        nlm.append(nl.astype(BF16))
        intra.append((aqk[h] * (e * mf_ref[MF_INCL])).astype(BF16))
        xb.append((nl * mf_ref[MF_PAIR] + mf_ref[MF_EYE]).astype(BF16))

    for lvl in range(n_levels):
        zb = [(_dot(xb[h], nlm[h] * mb_ref[lvl]) + mf_ref[MF_EYE]).astype(BF16) for h in heads]
        xb = [_dot(zb[h], xb[h]).astype(BF16) for h in heads]

    egc = [jnp.exp(gc[h]) for h in heads]
    rhs = [jnp.concatenate([vbeta[h], kbeta[h] * egc[h]], axis=1) for h in heads]
    sol = [rhs[h] + _dot(xb[h] * mb_ref[n_levels], rhs[h].astype(BF16)) for h in heads]
    state = [s_ref[h] for h in heads]
    sb = [state[h].astype(BF16) for h in heads]
    ws = [_dot(sol[h][:, dh:].astype(BF16), sb[h]) for h in heads]
    qs = [_dot((qn[h] * egc[h] * scale).astype(BF16), sb[h]) for h in heads]
    vnb = [(sol[h][:, :dh] - ws[h]).astype(BF16) for h in heads]
    o = [qs[h] + _dot(intra[h], vnb[h]) for h in heads]
    gn = gn_ref[...]
    for h in heads:
        total = gc[h][c - 1:c, :]
        kdt = (kn[h] * jnp.exp(total - gc[h])).T.astype(BF16)
        s_ref[h] = state[h] * jnp.exp(total) + _dot(kdt, vnb[h])
    for h in heads:
        sl = slice(h * dh, (h + 1) * dh)
        on = o[h] * lax.rsqrt(jnp.mean(o[h] * o[h], axis=-1, keepdims=True) + RMS_EPS) * gn
        o_ref[:, sl] = (on * _silu(z_ref[:, sl])).astype(BF16)


def gdn(cbuf, gates, conv_w, alog_row, dtb_row, norm_g, batch, chunk=GDN_C):
    t = cbuf.shape[0]
    nc = t // batch // chunk
    hb = chunk // V7X_SUBLANES
    w3 = 3 * C_WIDTH
    mf, incl2, mb = _gdn_masks(chunk)
    kern = functools.partial(_gdn_kernel, chunk=chunk)
    row = lambda w: pl.BlockSpec((chunk, w), lambda b, i: (b * nc + i, 0))
    const = lambda a: pl.BlockSpec(a.shape, lambda b, i: (0,) * a.ndim)
    return pl.pallas_call(
        kern,
        grid=(batch, nc),
        in_specs=[
            row(w3),
            pl.BlockSpec((V7X_SUBLANES, w3), lambda b, i: (jnp.maximum((b * nc + i) * hb - 1, 0), 0)),
            const(conv_w),
            row(V7X_LANES),
            pl.BlockSpec((1, V7X_LANES), lambda b, i: (0, 0)),
            pl.BlockSpec((1, V7X_LANES), lambda b, i: (0, 0)),
            pl.BlockSpec((chunk, C_WIDTH), lambda b, i: (b * nc + i, 3)),
            pl.BlockSpec((1, C_HEAD_DIM), lambda b, i: (0, 0)),
            const(mf), const(incl2), const(mb),
        ],
        out_specs=row(C_WIDTH),
        out_shape=jax.ShapeDtypeStruct((t, C_WIDTH), BF16),
        scratch_shapes=[pltpu.VMEM((chunk + V7X_SUBLANES, w3), F32),
                        pltpu.VMEM((chunk, w3), F32),
                        pltpu.VMEM((C_HEADS, C_HEAD_DIM, C_HEAD_DIM), F32)],
        compiler_params=_cparams(("parallel", "arbitrary")),
        name="gdn",
    )(cbuf, cbuf, conv_w, gates, alog_row, dtb_row, cbuf, norm_g.reshape(1, C_HEAD_DIM), mf, incl2, mb)


def _out_kernel(x_ref, oa_ref, ob_ref, oc_ref, w_ref, g_ref, b_ref, o_ref):
    mix = _dot(oa_ref[...], w_ref[0:A_WIDTH, :])
    mix = mix + _dot(ob_ref[...], w_ref[A_WIDTH:A_WIDTH + B_WIDTH, :])
    mix = mix + _dot(oc_ref[...], w_ref[A_WIDTH + B_WIDTH:, :])
    y = DEEPNORM_ALPHA * x_ref[...] + mix
    o_ref[...] = _layer_norm(y, g_ref[...], b_ref[...])


def out_ln(x, oa, ob, oc, w_out, g, b, tm=OUT_TM):
    t, d = x.shape
    tm = min(tm, t)
    row = lambda w: pl.BlockSpec((tm, w), lambda i: (i, 0))
    return pl.pallas_call(
        _out_kernel,
        grid=(t // tm,),
        in_specs=[row(d), row(A_WIDTH), row(B_WIDTH), row(C_WIDTH),
                  pl.BlockSpec(w_out.shape, lambda i: (0, 0)),
                  pl.BlockSpec((1, d), lambda i: (0, 0)), pl.BlockSpec((1, d), lambda i: (0, 0))],
        out_specs=row(d),
        out_shape=jax.ShapeDtypeStruct((t, d), F32),
        compiler_params=_cparams(("parallel",)),
        name="out_ln",
    )(x, oa, ob, oc, w_out, g, b)


def _split_w_in(w):
    o = 0
    parts = {}
    for name, n in (("qa", 256), ("ka", 256), ("va", 256), ("qb", 256), ("kb", 256), ("vb", 256),
                    ("qkv", 3 * C_WIDTH), ("z", C_WIDTH), ("beta", C_HEADS), ("a", C_HEADS)):
        parts[name] = w[:, o:o + n]
        o += n
    wk = jnp.concatenate([parts["ka"], parts["kb"]], axis=1).astype(BF16)
    wt = jnp.concatenate([parts["qa"], parts["va"], parts["qb"], parts["vb"]], axis=1).T.astype(BF16)
    wc = jnp.concatenate([parts["qkv"], parts["z"]], axis=1).astype(BF16)
    pad = jnp.zeros((w.shape[0], V7X_LANES - 2 * C_HEADS), w.dtype)
    wg = jnp.concatenate([parts["beta"], parts["a"], pad], axis=1).astype(BF16)
    return wk, wt, wc, wg


def _gate_row(vals):
    row = jnp.zeros((1, V7X_LANES), F32)
    return row.at[0, C_HEADS:2 * C_HEADS].set(vals.astype(F32))


def kernel(x, ffn1_w_gu, ffn1_w_down, ffn2_w_gu, ffn2_w_down, ln_g, ln_b, w_in, conv_w, dn_a_log, dn_dt_bias,
           dn_norm_g, diff_lambda, diff_norm_g, sb_norm_g, w_out):
    batch, seq, d = x.shape
    h = x.reshape(batch * seq, d)
    for l in range(DEPTH):
        lambda_init = 0.8 - 0.6 * math.exp(-0.3 * l)
        ln = lambda i: (ln_g[l, i].reshape(1, d), ln_b[l, i].reshape(1, d))
        h = ffn_ln(h, ffn1_w_gu, ffn1_w_down, l, *ln(0))
        wk, wt, wc, wg = _split_w_in(w_in[l])
        ka, kb, qta, vta, qtb, vtb, cbuf, gates = in_proj(h, wk, wt, wc, wg)
        oa = diff_attn(qta, ka, vta, diff_lambda[l], diff_norm_g[l], batch, lambda_init)
        ob = stick_attn(qtb, kb, vtb, sb_norm_g[l], batch)
        oc = gdn(cbuf, gates, conv_w[l], _gate_row(dn_a_log[l]), _gate_row(dn_dt_bias[l]), dn_norm_g[l], batch)
        h = out_ln(h, oa, ob, oc, w_out[l].astype(BF16), *ln(1))
        h = ffn_ln(h, ffn2_w_gu, ffn2_w_down, l, *ln(2))
    return h.reshape(batch, seq, d)
```

```python
import functools
import math

import jax
import jax.numpy as jnp
import numpy as np
from jax import lax
from jax.experimental import pallas as pl
from jax.experimental.pallas import tpu as pltpu

F32 = jnp.float32
BF16 = jnp.bfloat16

DEPTH = 2
A_HEADS, A_QK_DIM, A_V_DIM = 4, 32, 64
B_HEADS, B_HEAD_DIM = 4, 64
C_HEADS, C_HEAD_DIM = 4, 128
CONV_K = 4
D_FF = 2816
A_WIDTH, B_WIDTH, C_WIDTH = 256, 256, 512
DEEPNORM_ALPHA = (2.0 * DEPTH) ** 0.25
LN_EPS = 1e-5
RMS_EPS = 1e-6
NEG_BIG = -1e30
LOG2_E = math.log2(math.e)
STICK_DEAD_LOG = 104.0

V7X_LANES = 128
V7X_SUBLANES = 8
V7X_VMEM_LIMIT_BYTES = 56 * 1024 * 1024
ONES_ROWS = 16

FFN_TM = 2048
FFN_TF = 256
PROJ_TM = 256
ATT_T = 256
GDN_C = 256
OUT_TM = 512
CONV_COLS = 512


def _cparams(sem):
    return pltpu.CompilerParams(dimension_semantics=sem, vmem_limit_bytes=V7X_VMEM_LIMIT_BYTES)


def _layer_norm(y, g, b):
    mu = jnp.mean(y, axis=-1, keepdims=True)
    d = y - mu
    var = jnp.mean(d * d, axis=-1, keepdims=True)
    return d * lax.rsqrt(var + LN_EPS) * g + b


def _sigmoid(x):
    return 0.5 * jnp.tanh(0.5 * x) + 0.5


def _silu(x):
    return x * _sigmoid(x)


def _softplus(x):
    return jnp.maximum(x, 0.0) + jnp.log1p(jnp.exp(-jnp.abs(x)))


def _dot(a, b):
    return jnp.dot(a, b, preferred_element_type=F32)


def _dot_nt(a, b):
    return lax.dot_general(a, b, (((1,), (1,)), ((), ())), preferred_element_type=F32)


def _split2_rows(x):
    hi = x.astype(BF16)
    lo = (x - hi.astype(F32)).astype(BF16)
    return jnp.concatenate([hi, lo], axis=0)


def _ffn_kernel(x_ref, wg_ref, wu_ref, wd_ref, g_ref, b_ref, o_ref, xb_ref):
    j = pl.program_id(1)

    @pl.when(j == 0)
    def _():
        xb_ref[...] = x_ref[...].astype(BF16)
        o_ref[...] = jnp.zeros_like(o_ref)

    xb = xb_ref[...]
    gate = _dot(xb, wg_ref[...].astype(BF16))
    up = _dot(xb, wu_ref[...].astype(BF16))
    act = (_silu(gate) * up).astype(BF16)
    o_ref[...] += _dot(act, wd_ref[...].astype(BF16))

    @pl.when(j == pl.num_programs(1) - 1)
    def _():
        y = DEEPNORM_ALPHA * x_ref[...] + 0.5 * o_ref[...]
        o_ref[...] = _layer_norm(y, g_ref[...], b_ref[...])


def ffn_ln(x, w_gu, w_down, layer, g, b, tm=FFN_TM, tf=FFN_TF):
    t, d = x.shape
    ff = w_down.shape[1]
    tm = min(tm, t)
    nf = ff // tf
    return pl.pallas_call(
        _ffn_kernel,
        grid=(t // tm, nf),
        in_specs=[
            pl.BlockSpec((tm, d), lambda i, j: (i, 0)),
            pl.BlockSpec((None, d, tf), lambda i, j: (layer, 0, j)),
            pl.BlockSpec((None, d, tf), lambda i, j: (layer, 0, j + nf)),
            pl.BlockSpec((None, tf, d), lambda i, j: (layer, j, 0)),
            pl.BlockSpec((1, d), lambda i, j: (0, 0)),
            pl.BlockSpec((1, d), lambda i, j: (0, 0)),
        ],
        out_specs=pl.BlockSpec((tm, d), lambda i, j: (i, 0)),
        out_shape=jax.ShapeDtypeStruct((t, d), F32),
        scratch_shapes=[pltpu.VMEM((tm, d), BF16)],
        compiler_params=_cparams(("parallel", "arbitrary")),
        name="ffn_ln",
    )(x, w_gu, w_gu, w_down, g, b)


def _proj_kernel(x_ref, wk_ref, wt_ref, wc_ref, wg_ref,
                 ka_ref, kb_ref, qta_ref, vta_ref, qtb_ref, vtb_ref, c_ref, gate_ref):
    xb = x_ref[...].astype(BF16)
    c_ref[...] = _dot(xb, wc_ref[...])
    k = _dot(xb, wk_ref[...])
    ka_ref[...] = k[:, :A_WIDTH].astype(BF16)
    kb_ref[...] = k[:, A_WIDTH:].astype(BF16)
    t = _dot_nt(wt_ref[...], xb)
    qta_ref[0] = (t[0:256] * (A_QK_DIM ** -0.5 * LOG2_E)).astype(BF16)
    vta_ref[0] = t[256:512].astype(BF16)
    qtb_ref[0] = (t[512:768] * (B_HEAD_DIM ** -0.5)).astype(BF16)
    vtb_ref[0] = t[768:1024].astype(BF16)
    gate_ref[...] = _dot(xb, wg_ref[...])


def in_proj(x, wk, wt, wc, wg, tm=PROJ_TM):
    t, d = x.shape
    nt = t // tm
    full = lambda a: pl.BlockSpec(a.shape, lambda i: (0,) * a.ndim)
    row = lambda w: pl.BlockSpec((tm, w), lambda i: (i, 0))
    tr = pl.BlockSpec((1, 256, tm), lambda i: (i, 0, 0))
    tr_shape = jax.ShapeDtypeStruct((nt, 256, tm), BF16)
    return pl.pallas_call(
        _proj_kernel,
        grid=(nt,),
        in_specs=[row(d), full(wk), full(wt), full(wc), full(wg)],
        out_specs=[row(256), row(256), tr, tr, tr, tr, row(wc.shape[1]), row(V7X_LANES)],
        out_shape=[
            jax.ShapeDtypeStruct((t, 256), BF16), jax.ShapeDtypeStruct((t, 256), BF16),
            tr_shape, tr_shape, tr_shape, tr_shape,
            jax.ShapeDtypeStruct((t, wc.shape[1]), F32), jax.ShapeDtypeStruct((t, V7X_LANES), F32),
        ],
        compiler_params=_cparams(("parallel",)),
        name="in_proj",
    )(x, wk, wt, wc, wg)


def _diff_attn_kernel(q_ref, k_ref, v_ref, lam_ref, g_ref, o_ref,
                      km_ref, m_ref, a_ref, acc_ref, s_ref, tm_ref, p_ref, ot_ref, *, lambda_init, tile, nq):
    qi = pl.program_id(1)
    nc = 2 * A_HEADS
    qt = q_ref[0]

    @pl.when(qi == 0)
    def _():
        lane = lax.broadcasted_iota(jnp.int32, (tile, 256), 1)
        for j in range(nq):
            kj = k_ref[j * tile:(j + 1) * tile, :]
            for c in range(nc):
                keep = (lane >= c * A_QK_DIM) & (lane < (c + 1) * A_QK_DIM)
                km_ref[j, c * tile:(c + 1) * tile, :] = jnp.where(keep, kj, jnp.zeros_like(kj))

    def scores(j, slot):
        s = _dot(km_ref[j], qt)
        s_ref[slot] = s
        for c in range(nc):
            tm_ref[slot, c] = jnp.max(s[c * tile:(c + 1) * tile], axis=0, keepdims=True)

    m_ref[...] = jnp.full(m_ref.shape, NEG_BIG, F32)
    acc_ref[...] = jnp.zeros_like(acc_ref)
    ones = jnp.ones((ONES_ROWS, tile), BF16)

    def step(cur, src, nxt, dst, masked):
        vt = v_ref[cur]
        if masked:
            key = lax.broadcasted_iota(jnp.int32, (tile, tile), 0)
            qry = lax.broadcasted_iota(jnp.int32, (tile, tile), 1)
            causal = key <= qry
        if nxt is not None:
            scores(nxt, dst)
        m_new = []
        for c in range(nc):
            if masked:
                tmax = jnp.max(jnp.where(causal, s_ref[src, c * tile:(c + 1) * tile, :], NEG_BIG),
                               axis=0, keepdims=True)
            else:
                tmax = tm_ref[src, c]
            m_old = m_ref[c]
            m_new.append(jnp.maximum(m_old, tmax))
            a_ref[c] = jnp.exp2(m_old - m_new[c])
            m_ref[c] = m_new[c]

        def pv(c):
            h = c // 2
            vt_ext = jnp.concatenate([vt[h * A_V_DIM:(h + 1) * A_V_DIM, :], ones], axis=0)
            acc_ref[c] = a_ref[c] * acc_ref[c] + _dot(vt_ext, p_ref[c])

        for c in range(nc):
            s = s_ref[src, c * tile:(c + 1) * tile, :]
            if masked:
                s = jnp.where(causal, s, NEG_BIG)
            p_ref[c] = jnp.exp2((s - m_new[c]).astype(BF16))
            if c > 0:
                pv(c - 1)
        pv(nc - 1)

    scores(0, 0)

    def body(jj, carry):
        j = 2 * jj
        step(j, 0, j + 1, 1, False)
        step(j + 1, 1, j + 2, 0, False)
        return carry

    lax.fori_loop(0, qi // 2, body, 0)

    @pl.when(qi % 2 == 0)
    def _():
        step(qi, 0, None, None, True)

    @pl.when(qi % 2 == 1)
    def _():
        step(qi - 1, 0, qi, 1, False)
        step(qi, 1, None, None, True)

    lf = lam_ref[...]
    lam = (jnp.exp(jnp.sum(lf[0:1] * lf[1:2], axis=-1, keepdims=True))
           - jnp.exp(jnp.sum(lf[2:3] * lf[3:4], axis=-1, keepdims=True)) + lambda_init)
    g = g_ref[...]
    dv = A_V_DIM
    for h in range(A_HEADS):
        a1 = acc_ref[2 * h]
        a2 = acc_ref[2 * h + 1]
        o = a1[:dv] / a1[dv:dv + 1] - lam * (a2[:dv] / a2[dv:dv + 1])
        ms = jnp.mean(o * o, axis=0, keepdims=True)
        ot_ref[h * A_V_DIM:(h + 1) * A_V_DIM, :] = o * lax.rsqrt(ms + RMS_EPS) * g * (1.0 - lambda_init)
    o_ref[...] = ot_ref[...].T.astype(BF16)


def diff_attn(qt, k, vt, diff_lambda, norm_g, batch, lambda_init, tile=ATT_T):
    t = k.shape[0]
    s = t // batch
    nq = s // tile
    nc = 2 * A_HEADS
    kern = functools.partial(_diff_attn_kernel, lambda_init=lambda_init, tile=tile, nq=nq)
    return pl.pallas_call(
        kern,
        grid=(batch, nq),
        in_specs=[
            pl.BlockSpec((1, 256, tile), lambda b, i: (b * nq + i, 0, 0)),
            pl.BlockSpec((s, 256), lambda b, i: (b, 0)),
            pl.BlockSpec((nq, 256, tile), lambda b, i: (b, 0, 0)),
            pl.BlockSpec(diff_lambda.shape, lambda b, i: (0, 0)),
            pl.BlockSpec((A_V_DIM, 1), lambda b, i: (0, 0)),
        ],
        out_specs=pl.BlockSpec((tile, 256), lambda b, i: (b * nq + i, 0)),
        out_shape=jax.ShapeDtypeStruct((t, 256), BF16),
        scratch_shapes=[
            pltpu.VMEM((nq, nc * tile, 256), BF16),
            pltpu.VMEM((nc, 1, tile), F32),
            pltpu.VMEM((nc, 1, tile), F32),
            pltpu.VMEM((nc, A_V_DIM + ONES_ROWS, tile), F32),
            pltpu.VMEM((2, nc * tile, tile), F32),
            pltpu.VMEM((2, nc, 1, tile), F32),
            pltpu.VMEM((nc, tile, tile), BF16),
            pltpu.VMEM((256, tile), F32),
        ],
        compiler_params=_cparams(("parallel", "arbitrary")),
        name="diff_attn",
    )(qt, k, vt, diff_lambda, norm_g.reshape(A_V_DIM, 1))


def _stick_attn_kernel(q_ref, k_ref, v_ref, g_ref, o_ref,
                       qm_ref, carry_ref, acc_ref, z_ref, hl_ref, t_ref, w_ref, ot_ref, *, tile):
    qi = pl.program_id(1)
    qt = q_ref[0]
    row = lax.broadcasted_iota(jnp.int32, qt.shape, 0)
    for h in range(B_HEADS):
        keep = (row >= h * B_HEAD_DIM) & (row < (h + 1) * B_HEAD_DIM)
        qm_ref[h] = jnp.where(keep, qt, jnp.zeros_like(qt))
    carry_ref[...] = jnp.zeros_like(carry_ref)
    acc_ref[...] = jnp.zeros_like(acc_ref)

    key = lax.broadcasted_iota(jnp.int32, (tile, tile), 0)
    qry = lax.broadcasted_iota(jnp.int32, (tile, tile), 1)
    upper = (qry > key).astype(BF16)
    upper2 = jnp.concatenate([upper, upper], axis=1)
    strict = key < qry

    def do_tile(j, masked):
        kb = k_ref[pl.ds(pl.multiple_of(j * tile, tile), tile), :]
        vt = v_ref[j]
        for h in range(B_HEADS):
            z_ref[h] = _dot(kb, qm_ref[h])
        for h in range(B_HEADS):
            z = z_ref[h]
            sp = jnp.maximum(z, 0.0) + jnp.log(1.0 + jnp.exp(-jnp.abs(z)))
            if masked:
                sp = jnp.where(strict, sp, 0.0)
            hi = sp.astype(BF16)
            hl_ref[h, 0:tile, :] = hi
            hl_ref[h, tile:, :] = (sp - hi.astype(F32)).astype(BF16)
            carry = carry_ref[h]
            z_ref[h] = z - sp - carry
            carry_ref[h] = carry + jnp.sum(sp, axis=0, keepdims=True)
        for h in range(B_HEADS):
            t_ref[h] = _dot(upper2, hl_ref[h])
        for h in range(B_HEADS):
            w = jnp.exp(z_ref[h] - t_ref[h])
            if masked:
                w = jnp.where(strict, w, 0.0)
            w_ref[h] = w.astype(BF16)
        for h in range(B_HEADS):
            acc_ref[h] += _dot(vt[h * B_HEAD_DIM:(h + 1) * B_HEAD_DIM, :], w_ref[h])

    do_tile(qi, True)

    def live():
        return jnp.min(carry_ref[...]) < STICK_DEAD_LOG

    def cond(state):
        jj, go = state
        return jnp.logical_and(jj < qi, go)

    def body(state):
        jj, _ = state
        do_tile(qi - 1 - jj, False)
        return jj + 1, live()

    lax.while_loop(cond, body, (jnp.int32(0), live()))

    g = g_ref[...]
    for h in range(B_HEADS):
        o = acc_ref[h]
        ms = jnp.mean(o * o, axis=0, keepdims=True)
        ot_ref[h * B_HEAD_DIM:(h + 1) * B_HEAD_DIM, :] = o * lax.rsqrt(ms + RMS_EPS) * g
    o_ref[...] = ot_ref[...].T.astype(BF16)


def stick_attn(qt, k, vt, norm_g, batch, tile=ATT_T):
    t = k.shape[0]
    s = t // batch
    nq = s // tile
    kern = functools.partial(_stick_attn_kernel, tile=tile)
    return pl.pallas_call(
        kern,
        grid=(batch, nq),
        in_specs=[
            pl.BlockSpec((1, 256, tile), lambda b, i: (b * nq + i, 0, 0)),
            pl.BlockSpec((s, 256), lambda b, i: (b, 0)),
            pl.BlockSpec((nq, 256, tile), lambda b, i: (b, 0, 0)),
            pl.BlockSpec((B_HEAD_DIM, 1), lambda b, i: (0, 0)),
        ],
        out_specs=pl.BlockSpec((tile, 256), lambda b, i: (b * nq + i, 0)),
        out_shape=jax.ShapeDtypeStruct((t, 256), BF16),
        scratch_shapes=[
            pltpu.VMEM((B_HEADS, 256, tile), BF16),
            pltpu.VMEM((B_HEADS, 1, tile), F32),
            pltpu.VMEM((B_HEADS, B_HEAD_DIM, tile), F32),
            pltpu.VMEM((B_HEADS, tile, tile), F32),
            pltpu.VMEM((B_HEADS, 2 * tile, tile), BF16),
            pltpu.VMEM((B_HEADS, tile, tile), F32),
            pltpu.VMEM((B_HEADS, tile, tile), BF16),
            pltpu.VMEM((256, tile), F32),
        ],
        compiler_params=_cparams(("parallel", "arbitrary")),
        name="stick_attn",
    )(qt, k, vt, norm_g.reshape(B_HEAD_DIM, 1))


MF_STRICT, MF_NSTRICT, MF_INCL, MF_EYE, MF_PAIR = range(5)


def _gdn_masks(c):
    r = np.arange(c)[:, None]
    q = np.arange(c)[None, :]
    strict = (q < r).astype(np.float32)
    incl = (q <= r).astype(np.float32)
    eye = (q == r).astype(np.float32)
    pair = ((r >> 1) == (q >> 1)).astype(np.float32)
    mf = np.stack([strict, -strict, incl, eye, pair])
    joins = []
    bs = 2
    while bs < c:
        sh = bs.bit_length() - 1
        joins.append((((r >> (sh + 1)) == (q >> (sh + 1))) & ((r >> sh) != (q >> sh))).astype(np.float32))
        bs *= 2
    joins.append(1.0 - eye)
    return (jnp.asarray(mf), jnp.asarray(np.concatenate([incl, incl], axis=1), dtype=BF16),
            jnp.asarray(np.stack(joins), dtype=BF16))


def _gdn_kernel(x_ref, halo_ref, cw_ref, gate_ref, alog_ref, dtb_ref, z_ref, gn_ref, mf_ref, incl2_ref, mb_ref,
                o_ref, xs_ref, y_ref, s_ref, *, chunk):
    ci = pl.program_id(1)

    @pl.when(ci == 0)
    def _():
        s_ref[...] = jnp.zeros_like(s_ref)

    c = chunk
    dh = C_HEAD_DIM
    heads = range(C_HEADS)
    n_levels = mb_ref.shape[0] - 1

    xs_ref[0:V7X_SUBLANES, :] = jnp.where(ci > 0, halo_ref[...], 0.0)
    xs_ref[V7X_SUBLANES:, :] = x_ref[...]
    first = V7X_SUBLANES - CONV_K + 1
    for c0 in range(0, 3 * C_WIDTH, CONV_COLS):
        cols = slice(c0, c0 + CONV_COLS)
        xs = xs_ref[:, cols]
        conv = cw_ref[CONV_K - 1:CONV_K, cols] * xs[V7X_SUBLANES:]
        for back in range(1, CONV_K):
            shifted = pltpu.roll(xs, back, axis=0)[V7X_SUBLANES:]
            conv = conv + cw_ref[CONV_K - 1 - back:CONV_K - back, cols] * shifted
        y_ref[:, cols] = _silu(conv)

    gates = gate_ref[...]
    beta_all = _sigmoid(gates)
    g_all = -jnp.exp(alog_ref[...]) * _softplus(gates + dtb_ref[...])
    scale = dh ** -0.5

    qn, kn, kbeta, vbeta, gwide = [], [], [], [], []
    for h in heads:
        q = y_ref[:, h * dh:(h + 1) * dh]
        k = y_ref[:, C_WIDTH + h * dh:C_WIDTH + (h + 1) * dh]
        v = y_ref[:, 2 * C_WIDTH + h * dh:2 * C_WIDTH + (h + 1) * dh]
        qn.append(q * lax.rsqrt(jnp.sum(q * q, axis=-1, keepdims=True) + RMS_EPS))
        k = k * lax.rsqrt(jnp.sum(k * k, axis=-1, keepdims=True) + RMS_EPS)
        kn.append(k)
        beta_b = jnp.broadcast_to(beta_all[:, h:h + 1], (c, dh))
        kbeta.append(k * beta_b)
        vbeta.append(v * beta_b)
        gwide.append(jnp.broadcast_to(g_all[:, C_HEADS + h:C_HEADS + h + 1], (c, c)))

    gc = [_dot(incl2_ref[...], _split2_rows(gwide[h][:, :dh])) for h in heads]
    gdiff = [_dot(incl2_ref[...], _split2_rows(gwide[h] * mf_ref[MF_STRICT])) for h in heads]
    kf = [kn[h].astype(BF16) for h in heads]
    akk = [_dot_nt(kbeta[h].astype(BF16), kf[h]) for h in heads]
    aqk = [_dot_nt((qn[h] * scale).astype(BF16), kf[h]) for h in heads]
    nlm, xb, intra = [], [], []
    for h in heads:
        e = jnp.exp(gdiff[h])
        nl = akk[h] * (e * mf_ref[MF_NSTRICT])
        nlm.append(nl.astype(BF16))
        intra.append((aqk[h] * (e * mf_ref[MF_INCL])).astype(BF16))
        xb.append((nl * mf_ref[MF_PAIR] + mf_ref[MF_EYE]).astype(BF16))

    for lvl in range(n_levels):
        zb = [(_dot(xb[h], nlm[h] * mb_ref[lvl]) + mf_ref[MF_EYE]).astype(BF16) for h in heads]
        xb = [_dot(zb[h], xb[h]).astype(BF16) for h in heads]

    egc = [jnp.exp(gc[h]) for h in heads]
    rhs = [jnp.concatenate([vbeta[h], kbeta[h] * egc[h]], axis=1) for h in heads]
    sol = [rhs[h] + _dot(xb[h] * mb_ref[n_levels], rhs[h].astype(BF16)) for h in heads]
    state = [s_ref[h] for h in heads]
    sb = [state[h].astype(BF16) for h in heads]
    ws = [_dot(sol[h][:, dh:].astype(BF16), sb[h]) for h in heads]
    qs = [_dot((qn[h] * egc[h] * scale).astype(BF16), sb[h]) for h in heads]
    vnb = [(sol[h][:, :dh] - ws[h]).astype(BF16) for h in heads]
    o = [qs[h] + _dot(intra[h], vnb[h]) for h in heads]
    gn = gn_ref[...]
    for h in heads:
        total = gc[h][c - 1:c, :]
        kdt = (kn[h] * jnp.exp(total - gc[h])).T.astype(BF16)
        s_ref[h] = state[h] * jnp.exp(total) + _dot(kdt, vnb[h])
    for h in heads:
        sl = slice(h * dh, (h + 1) * dh)
        on = o[h] * lax.rsqrt(jnp.mean(o[h] * o[h], axis=-1, keepdims=True) + RMS_EPS) * gn
        o_ref[:, sl] = (on * _silu(z_ref[:, sl])).astype(BF16)


def gdn(cbuf, gates, conv_w, alog_row, dtb_row, norm_g, batch, chunk=GDN_C):
    t = cbuf.shape[0]
    nc = t // batch // chunk
    hb = chunk // V7X_SUBLANES
    w3 = 3 * C_WIDTH
    mf, incl2, mb = _gdn_masks(chunk)
    kern = functools.partial(_gdn_kernel, chunk=chunk)
    row = lambda w: pl.BlockSpec((chunk, w), lambda b, i: (b * nc + i, 0))
    const = lambda a: pl.BlockSpec(a.shape, lambda b, i: (0,) * a.ndim)
    return pl.pallas_call(
        kern,
        grid=(batch, nc),
        in_specs=[
            row(w3),
            pl.BlockSpec((V7X_SUBLANES, w3), lambda b, i: (jnp.maximum((b * nc + i) * hb - 1, 0), 0)),
            const(conv_w),
            row(V7X_LANES),
            pl.BlockSpec((1, V7X_LANES), lambda b, i: (0, 0)),
            pl.BlockSpec((1, V7X_LANES), lambda b, i: (0, 0)),
            pl.BlockSpec((chunk, C_WIDTH), lambda b, i: (b * nc + i, 3)),
            pl.BlockSpec((1, C_HEAD_DIM), lambda b, i: (0, 0)),
            const(mf), const(incl2), const(mb),
        ],
        out_specs=row(C_WIDTH),
        out_shape=jax.ShapeDtypeStruct((t, C_WIDTH), BF16),
        scratch_shapes=[pltpu.VMEM((chunk + V7X_SUBLANES, w3), F32),
                        pltpu.VMEM((chunk, w3), F32),
                        pltpu.VMEM((C_HEADS, C_HEAD_DIM, C_HEAD_DIM), F32)],
        compiler_params=_cparams(("parallel", "arbitrary")),
        name="gdn",
    )(cbuf, cbuf, conv_w, gates, alog_row, dtb_row, cbuf, norm_g.reshape(1, C_HEAD_DIM), mf, incl2, mb)


def _out_kernel(x_ref, oa_ref, ob_ref, oc_ref, w_ref, g_ref, b_ref, o_ref):
    mix = _dot(oa_ref[...], w_ref[0:A_WIDTH, :])
    mix = mix + _dot(ob_ref[...], w_ref[A_WIDTH:A_WIDTH + B_WIDTH, :])
    mix = mix + _dot(oc_ref[...], w_ref[A_WIDTH + B_WIDTH:, :])
    y = DEEPNORM_ALPHA * x_ref[...] + mix
    o_ref[...] = _layer_norm(y, g_ref[...], b_ref[...])


def out_ln(x, oa, ob, oc, w_out, g, b, tm=OUT_TM):
    t, d = x.shape
    tm = min(tm, t)
    row = lambda w: pl.BlockSpec((tm, w), lambda i: (i, 0))
    return pl.pallas_call(
        _out_kernel,
        grid=(t // tm,),
        in_specs=[row(d), row(A_WIDTH), row(B_WIDTH), row(C_WIDTH),
                  pl.BlockSpec(w_out.shape, lambda i: (0, 0)),
                  pl.BlockSpec((1, d), lambda i: (0, 0)), pl.BlockSpec((1, d), lambda i: (0, 0))],
        out_specs=row(d),
        out_shape=jax.ShapeDtypeStruct((t, d), F32),
        compiler_params=_cparams(("parallel",)),
        name="out_ln",
    )(x, oa, ob, oc, w_out, g, b)


def _split_w_in(w):
    o = 0
    parts = {}
    for name, n in (("qa", 256), ("ka", 256), ("va", 256), ("qb", 256), ("kb", 256), ("vb", 256),
                    ("qkv", 3 * C_WIDTH), ("z", C_WIDTH), ("beta", C_HEADS), ("a", C_HEADS)):
        parts[name] = w[:, o:o + n]
        o += n
    wk = jnp.concatenate([parts["ka"], parts["kb"]], axis=1).astype(BF16)
    wt = jnp.concatenate([parts["qa"], parts["va"], parts["qb"], parts["vb"]], axis=1).T.astype(BF16)
    wc = jnp.concatenate([parts["qkv"], parts["z"]], axis=1).astype(BF16)
    pad = jnp.zeros((w.shape[0], V7X_LANES - 2 * C_HEADS), w.dtype)
    wg = jnp.concatenate([parts["beta"], parts["a"], pad], axis=1).astype(BF16)
    return wk, wt, wc, wg


def _gate_row(vals):
    row = jnp.zeros((1, V7X_LANES), F32)
    return row.at[0, C_HEADS:2 * C_HEADS].set(vals.astype(F32))


def kernel(x, ffn1_w_gu, ffn1_w_down, ffn2_w_gu, ffn2_w_down, ln_g, ln_b, w_in, conv_w, dn_a_log, dn_dt_bias,
           dn_norm_g, diff_lambda, diff_norm_g, sb_norm_g, w_out):
    batch, seq, d = x.shape
    h = x.reshape(batch * seq, d)
    for l in range(DEPTH):
        lambda_init = 0.8 - 0.6 * math.exp(-0.3 * l)
        ln = lambda i: (ln_g[l, i].reshape(1, d), ln_b[l, i].reshape(1, d))
        h = ffn_ln(h, ffn1_w_gu, ffn1_w_down, l, *ln(0))
        wk, wt, wc, wg = _split_w_in(w_in[l])
        ka, kb, qta, vta, qtb, vtb, cbuf, gates = in_proj(h, wk, wt, wc, wg)
        oa = diff_attn(qta, ka, vta, diff_lambda[l], diff_norm_g[l], batch, lambda_init)
        ob = stick_attn(qtb, kb, vtb, sb_norm_g[l], batch)
        oc = gdn(cbuf, gates, conv_w[l], _gate_row(dn_a_log[l]), _gate_row(dn_dt_bias[l]), dn_norm_g[l], batch)
        h = out_ln(h, oa, ob, oc, w_out[l].astype(BF16), *ln(1))
        h = ffn_ln(h, ffn2_w_gu, ffn2_w_down, l, *ln(2))
    return h.reshape(batch, seq, d)
```

```python
import functools
import math

import jax
import jax.numpy as jnp
import numpy as np
from jax import lax
from jax.experimental import pallas as pl
from jax.experimental.pallas import tpu as pltpu

F32 = jnp.float32
BF16 = jnp.bfloat16

DEPTH = 2
A_HEADS, A_QK_DIM, A_V_DIM = 4, 32, 64
B_HEADS, B_HEAD_DIM = 4, 64
C_HEADS, C_HEAD_DIM = 4, 128
CONV_K = 4
D_FF = 2816
A_WIDTH, B_WIDTH, C_WIDTH = 256, 256, 512
DEEPNORM_ALPHA = (2.0 * DEPTH) ** 0.25
LN_EPS = 1e-5
RMS_EPS = 1e-6
NEG_BIG = -1e30
LOG2_E = math.log2(math.e)
STICK_DEAD_LOG = 104.0

V7X_LANES = 128
V7X_SUBLANES = 8
V7X_VMEM_LIMIT_BYTES = 56 * 1024 * 1024
ONES_ROWS = 16

FFN_TM = 2048
FFN_TF = 256
PROJ_TM = 512
ATT_T = 256
GDN_C = 256
OUT_TM = 1024
CONV_COLS = 512
GDN_CHUNKS_PER_STEP = 2


def _cparams(sem):
    return pltpu.CompilerParams(dimension_semantics=sem, vmem_limit_bytes=V7X_VMEM_LIMIT_BYTES)


def _layer_norm(y, g, b):
    mu = jnp.mean(y, axis=-1, keepdims=True)
    d = y - mu
    var = jnp.mean(d * d, axis=-1, keepdims=True)
    return d * lax.rsqrt(var + LN_EPS) * g + b


def _sigmoid(x):
    return 0.5 * jnp.tanh(0.5 * x) + 0.5


def _silu(x):
    return x * _sigmoid(x)


def _softplus(x):
    return jnp.maximum(x, 0.0) + jnp.log1p(jnp.exp(-jnp.abs(x)))


def _dot(a, b):
    return jnp.dot(a, b, preferred_element_type=F32)


def _dot_nt(a, b):
    return lax.dot_general(a, b, (((1,), (1,)), ((), ())), preferred_element_type=F32)


def _split2_rows(x):
    hi = x.astype(BF16)
    lo = (x - hi.astype(F32)).astype(BF16)
    return jnp.concatenate([hi, lo], axis=0)


def _ffn_kernel(x_ref, wg_ref, wu_ref, wd_ref, g_ref, b_ref, o_ref, xb_ref):
    j = pl.program_id(1)

    def hidden_chunk(xb):
        gate = _dot(xb, wg_ref[...].astype(BF16))
        up = _dot(xb, wu_ref[...].astype(BF16))
        act = (_silu(gate) * up).astype(BF16)
        return _dot(act, wd_ref[...].astype(BF16))

    @pl.when(j == 0)
    def _():
        xb = x_ref[...].astype(BF16)
        xb_ref[...] = xb
        o_ref[...] = hidden_chunk(xb)

    @pl.when(j > 0)
    def _():
        o_ref[...] += hidden_chunk(xb_ref[...])

    @pl.when(j == pl.num_programs(1) - 1)
    def _():
        y = DEEPNORM_ALPHA * x_ref[...] + 0.5 * o_ref[...]
        o_ref[...] = _layer_norm(y, g_ref[...], b_ref[...])


def ffn_ln(x, w_gu, w_down, layer, g, b, tm=FFN_TM, tf=FFN_TF):
    t, d = x.shape
    ff = w_down.shape[1]
    tm = min(tm, t)
    nf = ff // tf
    return pl.pallas_call(
        _ffn_kernel,
        grid=(t // tm, nf),
        in_specs=[
            pl.BlockSpec((tm, d), lambda i, j: (i, 0)),
            pl.BlockSpec((None, d, tf), lambda i, j: (layer, 0, j)),
            pl.BlockSpec((None, d, tf), lambda i, j: (layer, 0, j + nf)),
            pl.BlockSpec((None, tf, d), lambda i, j: (layer, j, 0)),
            pl.BlockSpec((1, d), lambda i, j: (0, 0)),
            pl.BlockSpec((1, d), lambda i, j: (0, 0)),
        ],
        out_specs=pl.BlockSpec((tm, d), lambda i, j: (i, 0)),
        out_shape=jax.ShapeDtypeStruct((t, d), F32),
        scratch_shapes=[pltpu.VMEM((tm, d), BF16)],
        compiler_params=_cparams(("parallel", "arbitrary")),
        name="ffn_ln",
    )(x, w_gu, w_gu, w_down, g, b)


def _proj_kernel(x_ref, wk_ref, wt_ref, wc_ref, wg_ref,
                 ka_ref, kb_ref, qta_ref, vta_ref, qtb_ref, vtb_ref, c_ref, gate_ref):
    xb = x_ref[...].astype(BF16)
    c_ref[...] = _dot(xb, wc_ref[...])
    k = _dot(xb, wk_ref[...])
    ka_ref[...] = k[:, :A_WIDTH].astype(BF16)
    kb_ref[...] = k[:, A_WIDTH:].astype(BF16)
    t = _dot_nt(wt_ref[...], xb)
    for s in range(qta_ref.shape[0]):
        cs = slice(s * ATT_T, (s + 1) * ATT_T)
        qta_ref[s] = (t[0:256, cs] * (A_QK_DIM ** -0.5 * LOG2_E)).astype(BF16)
        vta_ref[s] = t[256:512, cs].astype(BF16)
        qtb_ref[s] = (t[512:768, cs] * (B_HEAD_DIM ** -0.5)).astype(BF16)
        vtb_ref[s] = t[768:1024, cs].astype(BF16)
    gate_ref[...] = _dot(xb, wg_ref[...])


def in_proj(x, wk, wt, wc, wg, tm=PROJ_TM):
    t, d = x.shape
    tm = min(tm, t)
    nt = t // tm
    full = lambda a: pl.BlockSpec(a.shape, lambda i: (0,) * a.ndim)
    row = lambda w: pl.BlockSpec((tm, w), lambda i: (i, 0))
    tr = pl.BlockSpec((tm // ATT_T, 256, ATT_T), lambda i: (i, 0, 0))
    tr_shape = jax.ShapeDtypeStruct((t // ATT_T, 256, ATT_T), BF16)
    return pl.pallas_call(
        _proj_kernel,
        grid=(nt,),
        in_specs=[row(d), full(wk), full(wt), full(wc), full(wg)],
        out_specs=[row(256), row(256), tr, tr, tr, tr, row(wc.shape[1]), row(V7X_LANES)],
        out_shape=[
            jax.ShapeDtypeStruct((t, 256), BF16), jax.ShapeDtypeStruct((t, 256), BF16),
            tr_shape, tr_shape, tr_shape, tr_shape,
            jax.ShapeDtypeStruct((t, wc.shape[1]), F32), jax.ShapeDtypeStruct((t, V7X_LANES), F32),
        ],
        compiler_params=_cparams(("parallel",)),
        name="in_proj",
    )(x, wk, wt, wc, wg)


def _diff_attn_kernel(q_ref, k_ref, v_ref, lam_ref, g_ref, o_ref,
                      km_ref, m_ref, a_ref, acc_ref, s_ref, tm_ref, p_ref, ot_ref, *, lambda_init, tile, nq):
    qi = pl.program_id(1)
    nc = 2 * A_HEADS
    qt = q_ref[0]

    @pl.when(qi == 0)
    def _():
        lane = lax.broadcasted_iota(jnp.int32, (tile, 256), 1)
        for j in range(nq):
            kj = k_ref[j * tile:(j + 1) * tile, :]
            for c in range(nc):
                keep = (lane >= c * A_QK_DIM) & (lane < (c + 1) * A_QK_DIM)
                km_ref[j, c * tile:(c + 1) * tile, :] = jnp.where(keep, kj, jnp.zeros_like(kj))

    def scores(j, slot):
        s = _dot(km_ref[j], qt)
        s_ref[slot] = s
        for c in range(nc):
            tm_ref[slot, c] = jnp.max(s[c * tile:(c + 1) * tile], axis=0, keepdims=True)

    m_ref[...] = jnp.full(m_ref.shape, NEG_BIG, F32)
    acc_ref[...] = jnp.zeros_like(acc_ref)
    ones = jnp.ones((ONES_ROWS, tile), BF16)

    def step(cur, src, nxt, dst, masked):
        vt = v_ref[cur]
        if masked:
            key = lax.broadcasted_iota(jnp.int32, (tile, tile), 0)
            qry = lax.broadcasted_iota(jnp.int32, (tile, tile), 1)
            causal = key <= qry
        if nxt is not None:
            scores(nxt, dst)
        m_new = []
        for c in range(nc):
            if masked:
                tmax = jnp.max(jnp.where(causal, s_ref[src, c * tile:(c + 1) * tile, :], NEG_BIG),
                               axis=0, keepdims=True)
            else:
                tmax = tm_ref[src, c]
            m_old = m_ref[c]
            m_new.append(jnp.maximum(m_old, tmax))
            a_ref[c] = jnp.exp2(m_old - m_new[c])
            m_ref[c] = m_new[c]

        def pv(c):
            h = c // 2
            vt_ext = jnp.concatenate([vt[h * A_V_DIM:(h + 1) * A_V_DIM, :], ones], axis=0)
            acc_ref[c] = a_ref[c] * acc_ref[c] + _dot(vt_ext, p_ref[c])

        for c in range(nc):
            s = s_ref[src, c * tile:(c + 1) * tile, :]
            if masked:
                s = jnp.where(causal, s, NEG_BIG)
            p_ref[c] = jnp.exp2((s - m_new[c]).astype(BF16))
            if c > 0:
                pv(c - 1)
        pv(nc - 1)

    scores(0, 0)

    def body(jj, carry):
        j = 2 * jj
        step(j, 0, j + 1, 1, False)
        step(j + 1, 1, j + 2, 0, False)
        return carry

    lax.fori_loop(0, qi // 2, body, 0)

    @pl.when(qi % 2 == 0)
    def _():
        step(qi, 0, None, None, True)

    @pl.when(qi % 2 == 1)
    def _():
        step(qi - 1, 0, qi, 1, False)
        step(qi, 1, None, None, True)

    lf = lam_ref[...]
    lam = (jnp.exp(jnp.sum(lf[0:1] * lf[1:2], axis=-1, keepdims=True))
           - jnp.exp(jnp.sum(lf[2:3] * lf[3:4], axis=-1, keepdims=True)) + lambda_init)
    g = g_ref[...]
    dv = A_V_DIM
    for h in range(A_HEADS):
        a1 = acc_ref[2 * h]
        a2 = acc_ref[2 * h + 1]
        o = a1[:dv] / a1[dv:dv + 1] - lam * (a2[:dv] / a2[dv:dv + 1])
        ms = jnp.mean(o * o, axis=0, keepdims=True)
        ot_ref[h * A_V_DIM:(h + 1) * A_V_DIM, :] = o * lax.rsqrt(ms + RMS_EPS) * g * (1.0 - lambda_init)
    o_ref[...] = ot_ref[...].T.astype(BF16)


def diff_attn(qt, k, vt, diff_lambda, norm_g, batch, lambda_init, tile=ATT_T):
    t = k.shape[0]
    s = t // batch
    nq = s // tile
    nc = 2 * A_HEADS
    kern = functools.partial(_diff_attn_kernel, lambda_init=lambda_init, tile=tile, nq=nq)
    return pl.pallas_call(
        kern,
        grid=(batch, nq),
        in_specs=[
            pl.BlockSpec((1, 256, tile), lambda b, i: (b * nq + i, 0, 0)),
            pl.BlockSpec((s, 256), lambda b, i: (b, 0)),
            pl.BlockSpec((nq, 256, tile), lambda b, i: (b, 0, 0)),
            pl.BlockSpec(diff_lambda.shape, lambda b, i: (0, 0)),
            pl.BlockSpec((A_V_DIM, 1), lambda b, i: (0, 0)),
        ],
        out_specs=pl.BlockSpec((tile, 256), lambda b, i: (b * nq + i, 0)),
        out_shape=jax.ShapeDtypeStruct((t, 256), BF16),
        scratch_shapes=[
            pltpu.VMEM((nq, nc * tile, 256), BF16),
            pltpu.VMEM((nc, 1, tile), F32),
            pltpu.VMEM((nc, 1, tile), F32),
            pltpu.VMEM((nc, A_V_DIM + ONES_ROWS, tile), F32),
            pltpu.VMEM((2, nc * tile, tile), F32),
            pltpu.VMEM((2, nc, 1, tile), F32),
            pltpu.VMEM((nc, tile, tile), BF16),
            pltpu.VMEM((256, tile), F32),
        ],
        compiler_params=_cparams(("parallel", "arbitrary")),
        name="diff_attn",
    )(qt, k, vt, diff_lambda, norm_g.reshape(A_V_DIM, 1))


def _stick_attn_kernel(q_ref, k_ref, v_ref, g_ref, o_ref,
                       qm_ref, carry_ref, acc_ref, z_ref, hl_ref, t_ref, w_ref, ot_ref, *, tile):
    qi = pl.program_id(1)
    qt = q_ref[0]
    row = lax.broadcasted_iota(jnp.int32, qt.shape, 0)
    for h in range(B_HEADS):
        keep = (row >= h * B_HEAD_DIM) & (row < (h + 1) * B_HEAD_DIM)
        qm_ref[h] = jnp.where(keep, qt, jnp.zeros_like(qt))
    carry_ref[...] = jnp.zeros_like(carry_ref)
    acc_ref[...] = jnp.zeros_like(acc_ref)

    key = lax.broadcasted_iota(jnp.int32, (tile, tile), 0)
    qry = lax.broadcasted_iota(jnp.int32, (tile, tile), 1)
    upper = (qry > key).astype(BF16)
    upper2 = jnp.concatenate([upper, upper], axis=1)
    strict = key < qry

    def do_tile(j, masked):
        kb = k_ref[pl.ds(pl.multiple_of(j * tile, tile), tile), :]
        vt = v_ref[j]
        for h in range(B_HEADS):
            z_ref[h] = _dot(kb, qm_ref[h])
        for h in range(B_HEADS):
            z = z_ref[h]
            sp = jnp.maximum(z, 0.0) + jnp.log(1.0 + jnp.exp(-jnp.abs(z)))
            if masked:
                sp = jnp.where(strict, sp, 0.0)
            hi = sp.astype(BF16)
            hl_ref[h, 0:tile, :] = hi
            hl_ref[h, tile:, :] = (sp - hi.astype(F32)).astype(BF16)
            carry = carry_ref[h]
            z_ref[h] = z - sp - carry
            carry_ref[h] = carry + jnp.sum(sp, axis=0, keepdims=True)
        for h in range(B_HEADS):
            t_ref[h] = _dot(upper2, hl_ref[h])
        for h in range(B_HEADS):
            w = jnp.exp(z_ref[h] - t_ref[h])
            if masked:
                w = jnp.where(strict, w, 0.0)
            w_ref[h] = w.astype(BF16)
        for h in range(B_HEADS):
            acc_ref[h] += _dot(vt[h * B_HEAD_DIM:(h + 1) * B_HEAD_DIM, :], w_ref[h])

    do_tile(qi, True)

    def live():
        return jnp.min(carry_ref[...]) < STICK_DEAD_LOG

    def cond(state):
        jj, go = state
        return jnp.logical_and(jj < qi, go)

    def body(state):
        jj, _ = state
        do_tile(qi - 1 - jj, False)
        return jj + 1, live()

    lax.while_loop(cond, body, (jnp.int32(0), live()))

    g = g_ref[...]
    for h in range(B_HEADS):
        o = acc_ref[h]
        ms = jnp.mean(o * o, axis=0, keepdims=True)
        ot_ref[h * B_HEAD_DIM:(h + 1) * B_HEAD_DIM, :] = o * lax.rsqrt(ms + RMS_EPS) * g
    o_ref[...] = ot_ref[...].T.astype(BF16)


def stick_attn(qt, k, vt, norm_g, batch, tile=ATT_T):
    t = k.shape[0]
    s = t // batch
    nq = s // tile
    kern = functools.partial(_stick_attn_kernel, tile=tile)
    return pl.pallas_call(
        kern,
        grid=(batch, nq),
        in_specs=[
            pl.BlockSpec((1, 256, tile), lambda b, i: (b * nq + i, 0, 0)),
            pl.BlockSpec((s, 256), lambda b, i: (b, 0)),
            pl.BlockSpec((nq, 256, tile), lambda b, i: (b, 0, 0)),
            pl.BlockSpec((B_HEAD_DIM, 1), lambda b, i: (0, 0)),
        ],
        out_specs=pl.BlockSpec((tile, 256), lambda b, i: (b * nq + i, 0)),
        out_shape=jax.ShapeDtypeStruct((t, 256), BF16),
        scratch_shapes=[
            pltpu.VMEM((B_HEADS, 256, tile), BF16),
            pltpu.VMEM((B_HEADS, 1, tile), F32),
            pltpu.VMEM((B_HEADS, B_HEAD_DIM, tile), F32),
            pltpu.VMEM((B_HEADS, tile, tile), F32),
            pltpu.VMEM((B_HEADS, 2 * tile, tile), BF16),
            pltpu.VMEM((B_HEADS, tile, tile), F32),
            pltpu.VMEM((B_HEADS, tile, tile), BF16),
            pltpu.VMEM((256, tile), F32),
        ],
        compiler_params=_cparams(("parallel", "arbitrary")),
        name="stick_attn",
    )(qt, k, vt, norm_g.reshape(B_HEAD_DIM, 1))


MF_STRICT, MF_NSTRICT, MF_INCL, MF_EYE, MF_PAIR = range(5)


def _gdn_masks(c):
    r = np.arange(c)[:, None]
    q = np.arange(c)[None, :]
    strict = (q < r).astype(np.float32)
    incl = (q <= r).astype(np.float32)
    eye = (q == r).astype(np.float32)
    pair = ((r >> 1) == (q >> 1)).astype(np.float32)
    mf = np.stack([strict, -strict, incl, eye, pair])
    joins = []
    bs = 2
    while bs < c:
        sh = bs.bit_length() - 1
        joins.append((((r >> (sh + 1)) == (q >> (sh + 1))) & ((r >> sh) != (q >> sh))).astype(np.float32))
        bs *= 2
    joins.append(1.0 - eye)
    return (jnp.asarray(mf), jnp.asarray(np.concatenate([incl, incl], axis=1), dtype=BF16),
            jnp.asarray(np.stack(joins), dtype=BF16))


def _gdn_kernel(x_ref, halo_ref, cw_ref, gate_ref, alog_ref, dtb_ref, z_ref, gn_ref, mf_ref, incl2_ref, mb_ref,
                o_ref, xs_ref, y_ref, s_ref, *, chunk, per_step):
    gi = pl.program_id(1)

    @pl.when(gi == 0)
    def _():
        s_ref[...] = jnp.zeros_like(s_ref)

    c = chunk
    dh = C_HEAD_DIM
    n_levels = mb_ref.shape[0] - 1

    xs_ref[0:V7X_SUBLANES, :] = jnp.where(gi > 0, halo_ref[...], 0.0)
    xs_ref[V7X_SUBLANES:, :] = x_ref[...]
    for c0 in range(0, 3 * C_WIDTH, CONV_COLS):
        cols = slice(c0, c0 + CONV_COLS)
        xs = xs_ref[:, cols]
        conv = cw_ref[CONV_K - 1:CONV_K, cols] * xs[V7X_SUBLANES:]
        for back in range(1, CONV_K):
            shifted = pltpu.roll(xs, back, axis=0)[V7X_SUBLANES:]
            conv = conv + cw_ref[CONV_K - 1 - back:CONV_K - back, cols] * shifted
        y_ref[:, cols] = _silu(conv)

    gates = gate_ref[...]
    beta_all = _sigmoid(gates)
    g_all = -jnp.exp(alog_ref[...]) * _softplus(gates + dtb_ref[...])
    scale = dh ** -0.5

    chains = [(ck, h) for ck in range(per_step) for h in range(C_HEADS)]
    qn, kn, kbeta, vbeta, gwide = {}, {}, {}, {}, {}
    for key in chains:
        ck, h = key
        rs = slice(ck * c, (ck + 1) * c)
        q = y_ref[rs, h * dh:(h + 1) * dh]
        k = y_ref[rs, C_WIDTH + h * dh:C_WIDTH + (h + 1) * dh]
        v = y_ref[rs, 2 * C_WIDTH + h * dh:2 * C_WIDTH + (h + 1) * dh]
        qn[key] = q * lax.rsqrt(jnp.sum(q * q, axis=-1, keepdims=True) + RMS_EPS)
        k = k * lax.rsqrt(jnp.sum(k * k, axis=-1, keepdims=True) + RMS_EPS)
        kn[key] = k
        beta_b = jnp.broadcast_to(beta_all[rs, h:h + 1], (c, dh))
        kbeta[key] = k * beta_b
        vbeta[key] = v * beta_b
        gwide[key] = jnp.broadcast_to(g_all[rs, C_HEADS + h:C_HEADS + h + 1], (c, c))

    gc = {key: _dot(incl2_ref[...], _split2_rows(gwide[key][:, :dh])) for key in chains}
    gdiff = {key: _dot(incl2_ref[...], _split2_rows(gwide[key] * mf_ref[MF_STRICT])) for key in chains}
    kf = {key: kn[key].astype(BF16) for key in chains}
    akk = {key: _dot_nt(kbeta[key].astype(BF16), kf[key]) for key in chains}
    aqk = {key: _dot_nt((qn[key] * scale).astype(BF16), kf[key]) for key in chains}
    nlm, xb, intra = {}, {}, {}
    for key in chains:
        e = jnp.exp(gdiff[key])
        nl = akk[key] * (e * mf_ref[MF_NSTRICT])
        nlm[key] = nl.astype(BF16)
        intra[key] = (aqk[key] * (e * mf_ref[MF_INCL])).astype(BF16)
        xb[key] = (nl * mf_ref[MF_PAIR] + mf_ref[MF_EYE]).astype(BF16)

    for lvl in range(n_levels):
        zb = {key: (_dot(xb[key], nlm[key] * mb_ref[lvl]) + mf_ref[MF_EYE]).astype(BF16) for key in chains}
        xb = {key: _dot(zb[key], xb[key]).astype(BF16) for key in chains}

    egc = {key: jnp.exp(gc[key]) for key in chains}
    rhs = {key: jnp.concatenate([vbeta[key], kbeta[key] * egc[key]], axis=1) for key in chains}
    sol = {key: rhs[key] + _dot(xb[key] * mb_ref[n_levels], rhs[key].astype(BF16)) for key in chains}
    wb = {key: sol[key][:, dh:].astype(BF16) for key in chains}
    qg = {key: (qn[key] * egc[key] * scale).astype(BF16) for key in chains}
    kdt, etot = {}, {}
    for key in chains:
        total = gc[key][c - 1:c, :]
        kdt[key] = (kn[key] * jnp.exp(total - gc[key])).T.astype(BF16)
        etot[key] = jnp.exp(total)

    gn = gn_ref[...]
    for ck in range(per_step):
        keys = [(ck, h) for h in range(C_HEADS)]
        rs = slice(ck * c, (ck + 1) * c)
        state = {key: s_ref[key[1]] for key in keys}
        sb = {key: state[key].astype(BF16) for key in keys}
        ws = {key: _dot(wb[key], sb[key]) for key in keys}
        qs = {key: _dot(qg[key], sb[key]) for key in keys}
        vnb = {key: (sol[key][:, :dh] - ws[key]).astype(BF16) for key in keys}
        o = {key: qs[key] + _dot(intra[key], vnb[key]) for key in keys}
        for key in keys:
            s_ref[key[1]] = state[key] * etot[key] + _dot(kdt[key], vnb[key])
        for key in keys:
            sl = slice(key[1] * dh, (key[1] + 1) * dh)
            on = o[key] * lax.rsqrt(jnp.mean(o[key] * o[key], axis=-1, keepdims=True) + RMS_EPS) * gn
            o_ref[rs, sl] = (on * _silu(z_ref[rs, sl])).astype(BF16)


def gdn(cbuf, gates, conv_w, alog_row, dtb_row, norm_g, batch, chunk=GDN_C, per_step=GDN_CHUNKS_PER_STEP):
    t = cbuf.shape[0]
    rows = chunk * per_step
    ng = t // batch // rows
    hb = rows // V7X_SUBLANES
    w3 = 3 * C_WIDTH
    mf, incl2, mb = _gdn_masks(chunk)
    kern = functools.partial(_gdn_kernel, chunk=chunk, per_step=per_step)
    row = lambda w: pl.BlockSpec((rows, w), lambda b, i: (b * ng + i, 0))
    const = lambda a: pl.BlockSpec(a.shape, lambda b, i: (0,) * a.ndim)
    return pl.pallas_call(
        kern,
        grid=(batch, ng),
        in_specs=[
            row(w3),
            pl.BlockSpec((V7X_SUBLANES, w3), lambda b, i: (jnp.maximum((b * ng + i) * hb - 1, 0), 0)),
            const(conv_w),
            row(V7X_LANES),
            pl.BlockSpec((1, V7X_LANES), lambda b, i: (0, 0)),
            pl.BlockSpec((1, V7X_LANES), lambda b, i: (0, 0)),
            pl.BlockSpec((rows, C_WIDTH), lambda b, i: (b * ng + i, 3)),
            pl.BlockSpec((1, C_HEAD_DIM), lambda b, i: (0, 0)),
            const(mf), const(incl2), const(mb),
        ],
        out_specs=row(C_WIDTH),
        out_shape=jax.ShapeDtypeStruct((t, C_WIDTH), BF16),
        scratch_shapes=[pltpu.VMEM((rows + V7X_SUBLANES, w3), F32),
                        pltpu.VMEM((rows, w3), F32),
                        pltpu.VMEM((C_HEADS, C_HEAD_DIM, C_HEAD_DIM), F32)],
        compiler_params=_cparams(("parallel", "arbitrary")),
        name="gdn",
    )(cbuf, cbuf, conv_w, gates, alog_row, dtb_row, cbuf, norm_g.reshape(1, C_HEAD_DIM), mf, incl2, mb)


def _out_kernel(x_ref, oa_ref, ob_ref, oc_ref, w_ref, g_ref, b_ref, o_ref):
    mix = _dot(oa_ref[...], w_ref[0:A_WIDTH, :])
    mix = mix + _dot(ob_ref[...], w_ref[A_WIDTH:A_WIDTH + B_WIDTH, :])
    mix = mix + _dot(oc_ref[...], w_ref[A_WIDTH + B_WIDTH:, :])
    y = DEEPNORM_ALPHA * x_ref[...] + mix
    o_ref[...] = _layer_norm(y, g_ref[...], b_ref[...])


def out_ln(x, oa, ob, oc, w_out, g, b, tm=OUT_TM):
    t, d = x.shape
    tm = min(tm, t)
    row = lambda w: pl.BlockSpec((tm, w), lambda i: (i, 0))
    return pl.pallas_call(
        _out_kernel,
        grid=(t // tm,),
        in_specs=[row(d), row(A_WIDTH), row(B_WIDTH), row(C_WIDTH),
                  pl.BlockSpec(w_out.shape, lambda i: (0, 0)),
                  pl.BlockSpec((1, d), lambda i: (0, 0)), pl.BlockSpec((1, d), lambda i: (0, 0))],
        out_specs=row(d),
        out_shape=jax.ShapeDtypeStruct((t, d), F32),
        compiler_params=_cparams(("parallel",)),
        name="out_ln",
    )(x, oa, ob, oc, w_out, g, b)


def _split_w_in(w):
    o = 0
    parts = {}
    for name, n in (("qa", 256), ("ka", 256), ("va", 256), ("qb", 256), ("kb", 256), ("vb", 256),
                    ("qkv", 3 * C_WIDTH), ("z", C_WIDTH), ("beta", C_HEADS), ("a", C_HEADS)):
        parts[name] = w[:, o:o + n]
        o += n
    wk = jnp.concatenate([parts["ka"], parts["kb"]], axis=1).astype(BF16)
    wt = jnp.concatenate([parts["qa"], parts["va"], parts["qb"], parts["vb"]], axis=1).T.astype(BF16)
    wc = jnp.concatenate([parts["qkv"], parts["z"]], axis=1).astype(BF16)
    pad = jnp.zeros((w.shape[0], V7X_LANES - 2 * C_HEADS), w.dtype)
    wg = jnp.concatenate([parts["beta"], parts["a"], pad], axis=1).astype(BF16)
    return wk, wt, wc, wg


def _gate_row(vals):
    row = jnp.zeros((1, V7X_LANES), F32)
    return row.at[0, C_HEADS:2 * C_HEADS].set(vals.astype(F32))


def kernel(x, ffn1_w_gu, ffn1_w_down, ffn2_w_gu, ffn2_w_down, ln_g, ln_b, w_in, conv_w, dn_a_log, dn_dt_bias,
           dn_norm_g, diff_lambda, diff_norm_g, sb_norm_g, w_out):
    batch, seq, d = x.shape
    h = x.reshape(batch * seq, d)
    for l in range(DEPTH):
        lambda_init = 0.8 - 0.6 * math.exp(-0.3 * l)
        ln = lambda i: (ln_g[l, i].reshape(1, d), ln_b[l, i].reshape(1, d))
        h = ffn_ln(h, ffn1_w_gu, ffn1_w_down, l, *ln(0))
        wk, wt, wc, wg = _split_w_in(w_in[l])
        ka, kb, qta, vta, qtb, vtb, cbuf, gates = in_proj(h, wk, wt, wc, wg)
        oa = diff_attn(qta, ka, vta, diff_lambda[l], diff_norm_g[l], batch, lambda_init)
        ob = stick_attn(qtb, kb, vtb, sb_norm_g[l], batch)
        oc = gdn(cbuf, gates, conv_w[l], _gate_row(dn_a_log[l]), _gate_row(dn_dt_bias[l]), dn_norm_g[l], batch)
        h = out_ln(h, oa, ob, oc, w_out[l].astype(BF16), *ln(1))
        h = ffn_ln(h, ffn2_w_gu, ffn2_w_down, l, *ln(2))
    return h.reshape(batch, seq, d)
```

```python
import functools
import math

import jax
import jax.numpy as jnp
import numpy as np
from jax import lax
from jax.experimental import pallas as pl
from jax.experimental.pallas import tpu as pltpu

F32 = jnp.float32
BF16 = jnp.bfloat16

DEPTH = 2
A_HEADS, A_QK_DIM, A_V_DIM = 4, 32, 64
B_HEADS, B_HEAD_DIM = 4, 64
C_HEADS, C_HEAD_DIM = 4, 128
CONV_K = 4
D_FF = 2816
A_WIDTH, B_WIDTH, C_WIDTH = 256, 256, 512
DEEPNORM_ALPHA = (2.0 * DEPTH) ** 0.25
LN_EPS = 1e-5
RMS_EPS = 1e-6
NEG_BIG = -1e30
LOG2_E = math.log2(math.e)
STICK_DEAD_LOG = 104.0

V7X_LANES = 128
V7X_SUBLANES = 8
V7X_VMEM_LIMIT_BYTES = 56 * 1024 * 1024
ONES_ROWS = 16

FFN_TM = 2048
FFN_TF = 256
PROJ_TM = 512
ATT_T = 256
GDN_C = 256
OUT_TM = 1024
CONV_COLS = 512
GDN_CHUNKS_PER_STEP = 2


def _cparams(sem):
    return pltpu.CompilerParams(dimension_semantics=sem, vmem_limit_bytes=V7X_VMEM_LIMIT_BYTES)


def _layer_norm(y, g, b):
    mu = jnp.mean(y, axis=-1, keepdims=True)
    d = y - mu
    var = jnp.mean(d * d, axis=-1, keepdims=True)
    return d * lax.rsqrt(var + LN_EPS) * g + b


def _sigmoid(x):
    return 0.5 * jnp.tanh(0.5 * x) + 0.5


def _silu(x):
    return x * _sigmoid(x)


def _softplus(x):
    return jnp.maximum(x, 0.0) + jnp.log1p(jnp.exp(-jnp.abs(x)))


def _dot(a, b):
    return jnp.dot(a, b, preferred_element_type=F32)


def _dot_nt(a, b):
    return lax.dot_general(a, b, (((1,), (1,)), ((), ())), preferred_element_type=F32)


def _split2_rows(x):
    hi = x.astype(BF16)
    lo = (x - hi.astype(F32)).astype(BF16)
    return jnp.concatenate([hi, lo], axis=0)


def _ffn_kernel(x_ref, wg_ref, wu_ref, wd_ref, g_ref, b_ref, o_ref, xb_ref):
    j = pl.program_id(1)

    def hidden_chunk(xb):
        gate = _dot(xb, wg_ref[...].astype(BF16))
        up = _dot(xb, wu_ref[...].astype(BF16))
        act = (_silu(gate) * up).astype(BF16)
        return _dot(act, wd_ref[...].astype(BF16))

    @pl.when(j == 0)
    def _():
        xb = x_ref[...].astype(BF16)
        xb_ref[...] = xb
        o_ref[...] = hidden_chunk(xb)

    @pl.when(j > 0)
    def _():
        o_ref[...] += hidden_chunk(xb_ref[...])

    @pl.when(j == pl.num_programs(1) - 1)
    def _():
        y = DEEPNORM_ALPHA * x_ref[...] + 0.5 * o_ref[...]
        o_ref[...] = _layer_norm(y, g_ref[...], b_ref[...])


def ffn_ln(x, w_gu, w_down, layer, g, b, tm=FFN_TM, tf=FFN_TF):
    t, d = x.shape
    ff = w_down.shape[1]
    tm = min(tm, t)
    nf = ff // tf
    return pl.pallas_call(
        _ffn_kernel,
        grid=(t // tm, nf),
        in_specs=[
            pl.BlockSpec((tm, d), lambda i, j: (i, 0)),
            pl.BlockSpec((None, d, tf), lambda i, j: (layer, 0, j)),
            pl.BlockSpec((None, d, tf), lambda i, j: (layer, 0, j + nf)),
            pl.BlockSpec((None, tf, d), lambda i, j: (layer, j, 0)),
            pl.BlockSpec((1, d), lambda i, j: (0, 0)),
            pl.BlockSpec((1, d), lambda i, j: (0, 0)),
        ],
        out_specs=pl.BlockSpec((tm, d), lambda i, j: (i, 0)),
        out_shape=jax.ShapeDtypeStruct((t, d), F32),
        scratch_shapes=[pltpu.VMEM((tm, d), BF16)],
        compiler_params=_cparams(("parallel", "arbitrary")),
        name="ffn_ln",
    )(x, w_gu, w_gu, w_down, g, b)


def _proj_kernel(x_ref, wk_ref, wt_ref, wc_ref, wg_ref,
                 ka_ref, kb_ref, qta_ref, vta_ref, qtb_ref, vtb_ref, c_ref, gate_ref):
    xb = x_ref[...].astype(BF16)
    c_ref[...] = _dot(xb, wc_ref[...])
    k = _dot(xb, wk_ref[...])
    ka_ref[...] = k[:, :A_WIDTH].astype(BF16)
    kb_ref[...] = k[:, A_WIDTH:].astype(BF16)
    t = _dot_nt(wt_ref[...], xb)
    for s in range(qta_ref.shape[0]):
        cs = slice(s * ATT_T, (s + 1) * ATT_T)
        qta_ref[s] = (t[0:256, cs] * (A_QK_DIM ** -0.5 * LOG2_E)).astype(BF16)
        vta_ref[s] = t[256:512, cs].astype(BF16)
        qtb_ref[s] = (t[512:768, cs] * (B_HEAD_DIM ** -0.5)).astype(BF16)
        vtb_ref[s] = t[768:1024, cs].astype(BF16)
    gate_ref[...] = _dot(xb, wg_ref[...])


def in_proj(x, wk, wt, wc, wg, tm=PROJ_TM):
    t, d = x.shape
    tm = min(tm, t)
    nt = t // tm
    full = lambda a: pl.BlockSpec(a.shape, lambda i: (0,) * a.ndim)
    row = lambda w: pl.BlockSpec((tm, w), lambda i: (i, 0))
    tr = pl.BlockSpec((tm // ATT_T, 256, ATT_T), lambda i: (i, 0, 0))
    tr_shape = jax.ShapeDtypeStruct((t // ATT_T, 256, ATT_T), BF16)
    return pl.pallas_call(
        _proj_kernel,
        grid=(nt,),
        in_specs=[row(d), full(wk), full(wt), full(wc), full(wg)],
        out_specs=[row(256), row(256), tr, tr, tr, tr, row(wc.shape[1]), row(V7X_LANES)],
        out_shape=[
            jax.ShapeDtypeStruct((t, 256), BF16), jax.ShapeDtypeStruct((t, 256), BF16),
            tr_shape, tr_shape, tr_shape, tr_shape,
            jax.ShapeDtypeStruct((t, wc.shape[1]), F32), jax.ShapeDtypeStruct((t, V7X_LANES), F32),
        ],
        compiler_params=_cparams(("parallel",)),
        name="in_proj",
    )(x, wk, wt, wc, wg)


def _diff_attn_kernel(q_ref, k_ref, v_ref, lam_ref, g_ref, o_ref,
                      km_ref, m_ref, a_ref, acc_ref, s_ref, tm_ref, p_ref, ot_ref, *, lambda_init, tile, nq):
    qi = pl.program_id(1)
    nc = 2 * A_HEADS
    qt = q_ref[0]

    @pl.when(qi == 0)
    def _():
        lane = lax.broadcasted_iota(jnp.int32, (tile, 256), 1)
        for j in range(nq):
            kj = k_ref[j * tile:(j + 1) * tile, :]
            for c in range(nc):
                keep = (lane >= c * A_QK_DIM) & (lane < (c + 1) * A_QK_DIM)
                km_ref[j, c * tile:(c + 1) * tile, :] = jnp.where(keep, kj, jnp.zeros_like(kj))

    def scores(j, slot):
        s = _dot(km_ref[j], qt)
        s_ref[slot] = s
        for c in range(nc):
            tm_ref[slot, c] = jnp.max(s[c * tile:(c + 1) * tile], axis=0, keepdims=True)

    m_ref[...] = jnp.full(m_ref.shape, NEG_BIG, F32)
    acc_ref[...] = jnp.zeros_like(acc_ref)
    ones = jnp.ones((ONES_ROWS, tile), BF16)

    def step(cur, src, nxt, dst, masked):
        vt = v_ref[cur]
        if masked:
            key = lax.broadcasted_iota(jnp.int32, (tile, tile), 0)
            qry = lax.broadcasted_iota(jnp.int32, (tile, tile), 1)
            causal = key <= qry
        if nxt is not None:
            scores(nxt, dst)
        m_new = []
        for c in range(nc):
            if masked:
                tmax = jnp.max(jnp.where(causal, s_ref[src, c * tile:(c + 1) * tile, :], NEG_BIG),
                               axis=0, keepdims=True)
            else:
                tmax = tm_ref[src, c]
            m_old = m_ref[c]
            m_new.append(jnp.maximum(m_old, tmax))
            a_ref[c] = jnp.exp2(m_old - m_new[c])
            m_ref[c] = m_new[c]

        def pv(c):
            h = c // 2
            vt_ext = jnp.concatenate([vt[h * A_V_DIM:(h + 1) * A_V_DIM, :], ones], axis=0)
            acc_ref[c] = a_ref[c] * acc_ref[c] + _dot(vt_ext, p_ref[c])

        for c in range(nc):
            s = s_ref[src, c * tile:(c + 1) * tile, :]
            if masked:
                s = jnp.where(causal, s, NEG_BIG)
            p_ref[c] = jnp.exp2((s - m_new[c]).astype(BF16))
            if c > 0:
                pv(c - 1)
        pv(nc - 1)

    scores(0, 0)

    def body(jj, carry):
        j = 2 * jj
        step(j, 0, j + 1, 1, False)
        step(j + 1, 1, j + 2, 0, False)
        return carry

    lax.fori_loop(0, qi // 2, body, 0)

    @pl.when(qi % 2 == 0)
    def _():
        step(qi, 0, None, None, True)

    @pl.when(qi % 2 == 1)
    def _():
        step(qi - 1, 0, qi, 1, False)
        step(qi, 1, None, None, True)

    lf = lam_ref[...]
    lam = (jnp.exp(jnp.sum(lf[0:1] * lf[1:2], axis=-1, keepdims=True))
           - jnp.exp(jnp.sum(lf[2:3] * lf[3:4], axis=-1, keepdims=True)) + lambda_init)
    g = g_ref[...]
    dv = A_V_DIM
    for h in range(A_HEADS):
        a1 = acc_ref[2 * h]
        a2 = acc_ref[2 * h + 1]
        o = a1[:dv] / a1[dv:dv + 1] - lam * (a2[:dv] / a2[dv:dv + 1])
        ms = jnp.mean(o * o, axis=0, keepdims=True)
        ot_ref[h * A_V_DIM:(h + 1) * A_V_DIM, :] = o * lax.rsqrt(ms + RMS_EPS) * g * (1.0 - lambda_init)
    o_ref[...] = ot_ref[...].T.astype(BF16)


def diff_attn(qt, k, vt, diff_lambda, norm_g, batch, lambda_init, tile=ATT_T):
    t = k.shape[0]
    s = t // batch
    nq = s // tile
    nc = 2 * A_HEADS
    kern = functools.partial(_diff_attn_kernel, lambda_init=lambda_init, tile=tile, nq=nq)
    return pl.pallas_call(
        kern,
        grid=(batch, nq),
        in_specs=[
            pl.BlockSpec((1, 256, tile), lambda b, i: (b * nq + i, 0, 0)),
            pl.BlockSpec((s, 256), lambda b, i: (b, 0)),
            pl.BlockSpec((nq, 256, tile), lambda b, i: (b, 0, 0)),
            pl.BlockSpec(diff_lambda.shape, lambda b, i: (0, 0)),
            pl.BlockSpec((A_V_DIM, 1), lambda b, i: (0, 0)),
        ],
        out_specs=pl.BlockSpec((tile, 256), lambda b, i: (b * nq + i, 0)),
        out_shape=jax.ShapeDtypeStruct((t, 256), BF16),
        scratch_shapes=[
            pltpu.VMEM((nq, nc * tile, 256), BF16),
            pltpu.VMEM((nc, 1, tile), F32),
            pltpu.VMEM((nc, 1, tile), F32),
            pltpu.VMEM((nc, A_V_DIM + ONES_ROWS, tile), F32),
            pltpu.VMEM((2, nc * tile, tile), F32),
            pltpu.VMEM((2, nc, 1, tile), F32),
            pltpu.VMEM((nc, tile, tile), BF16),
            pltpu.VMEM((256, tile), F32),
        ],
        compiler_params=_cparams(("parallel", "arbitrary")),
        name="diff_attn",
    )(qt, k, vt, diff_lambda, norm_g.reshape(A_V_DIM, 1))


def _stick_attn_kernel(q_ref, k_ref, v_ref, g_ref, o_ref,
                       qm_ref, carry_ref, acc_ref, z_ref, hl_ref, t_ref, w_ref, ot_ref, *, tile):
    qi = pl.program_id(1)
    qt = q_ref[0]
    row = lax.broadcasted_iota(jnp.int32, qt.shape, 0)
    for h in range(B_HEADS):
        keep = (row >= h * B_HEAD_DIM) & (row < (h + 1) * B_HEAD_DIM)
        qm_ref[h] = jnp.where(keep, qt, jnp.zeros_like(qt))
    carry_ref[...] = jnp.zeros_like(carry_ref)
    acc_ref[...] = jnp.zeros_like(acc_ref)

    key = lax.broadcasted_iota(jnp.int32, (tile, tile), 0)
    qry = lax.broadcasted_iota(jnp.int32, (tile, tile), 1)
    upper = (qry > key).astype(BF16)
    upper2 = jnp.concatenate([upper, upper], axis=1)
    strict = key < qry

    def do_tiles(tiles):
        slots = range(len(tiles))
        for ti in slots:
            j = tiles[ti][0]
            kb = k_ref[pl.ds(pl.multiple_of(j * tile, tile), tile), :]
            for h in range(B_HEADS):
                z_ref[ti, h] = _dot(kb, qm_ref[h])
        csum = {}
        for ti in slots:
            for h in range(B_HEADS):
                z = z_ref[ti, h]
                sp = jnp.maximum(z, 0.0) + jnp.log(1.0 + jnp.exp(-jnp.abs(z)))
                if tiles[ti][1]:
                    sp = jnp.where(strict, sp, 0.0)
                hi = sp.astype(BF16)
                hl_ref[ti, h, 0:tile, :] = hi
                hl_ref[ti, h, tile:, :] = (sp - hi.astype(F32)).astype(BF16)
                z_ref[ti, h] = z - sp
                csum[ti, h] = jnp.sum(sp, axis=0, keepdims=True)
        before = {}
        for h in range(B_HEADS):
            carry = carry_ref[h]
            for ti in slots:
                before[ti, h] = carry
                carry = carry + csum[ti, h]
            carry_ref[h] = carry
        for ti in slots:
            for h in range(B_HEADS):
                t_ref[ti, h] = _dot(upper2, hl_ref[ti, h])
        for ti in slots:
            for h in range(B_HEADS):
                w = jnp.exp(z_ref[ti, h] - t_ref[ti, h] - before[ti, h])
                if tiles[ti][1]:
                    w = jnp.where(strict, w, 0.0)
                w_ref[ti, h] = w.astype(BF16)
        for ti in slots:
            vt = v_ref[tiles[ti][0]]
            for h in range(B_HEADS):
                acc_ref[h] += _dot(vt[h * B_HEAD_DIM:(h + 1) * B_HEAD_DIM, :], w_ref[ti, h])

    @pl.when(qi == 0)
    def _():
        do_tiles([(qi, True)])

    @pl.when(qi > 0)
    def _():
        do_tiles([(qi, True), (qi - 1, False)])

    def live():
        return jnp.min(carry_ref[...]) < STICK_DEAD_LOG

    def cond(state):
        jj, go = state
        return jnp.logical_and(jj < qi, go)

    def body(state):
        jj, _ = state
        do_tiles([(qi - 1 - jj, False)])
        return jj + 1, live()

    lax.while_loop(cond, body, (jnp.int32(1), live()))

    g = g_ref[...]
    for h in range(B_HEADS):
        o = acc_ref[h]
        ms = jnp.mean(o * o, axis=0, keepdims=True)
        ot_ref[h * B_HEAD_DIM:(h + 1) * B_HEAD_DIM, :] = o * lax.rsqrt(ms + RMS_EPS) * g
    o_ref[...] = ot_ref[...].T.astype(BF16)


def stick_attn(qt, k, vt, norm_g, batch, tile=ATT_T):
    t = k.shape[0]
    s = t // batch
    nq = s // tile
    kern = functools.partial(_stick_attn_kernel, tile=tile)
    return pl.pallas_call(
        kern,
        grid=(batch, nq),
        in_specs=[
            pl.BlockSpec((1, 256, tile), lambda b, i: (b * nq + i, 0, 0)),
            pl.BlockSpec((s, 256), lambda b, i: (b, 0)),
            pl.BlockSpec((nq, 256, tile), lambda b, i: (b, 0, 0)),
            pl.BlockSpec((B_HEAD_DIM, 1), lambda b, i: (0, 0)),
        ],
        out_specs=pl.BlockSpec((tile, 256), lambda b, i: (b * nq + i, 0)),
        out_shape=jax.ShapeDtypeStruct((t, 256), BF16),
        scratch_shapes=[
            pltpu.VMEM((B_HEADS, 256, tile), BF16),
            pltpu.VMEM((B_HEADS, 1, tile), F32),
            pltpu.VMEM((B_HEADS, B_HEAD_DIM, tile), F32),
            pltpu.VMEM((2, B_HEADS, tile, tile), F32),
            pltpu.VMEM((2, B_HEADS, 2 * tile, tile), BF16),
            pltpu.VMEM((2, B_HEADS, tile, tile), F32),
            pltpu.VMEM((2, B_HEADS, tile, tile), BF16),
            pltpu.VMEM((256, tile), F32),
        ],
        compiler_params=_cparams(("parallel", "arbitrary")),
        name="stick_attn",
    )(qt, k, vt, norm_g.reshape(B_HEAD_DIM, 1))


MF_STRICT, MF_NSTRICT, MF_INCL, MF_EYE, MF_PAIR = range(5)


def _gdn_masks(c):
    r = np.arange(c)[:, None]
    q = np.arange(c)[None, :]
    strict = (q < r).astype(np.float32)
    incl = (q <= r).astype(np.float32)
    eye = (q == r).astype(np.float32)
    pair = ((r >> 1) == (q >> 1)).astype(np.float32)
    mf = np.stack([strict, -strict, incl, eye, pair])
    joins = []
    bs = 2
    while bs < c:
        sh = bs.bit_length() - 1
        joins.append((((r >> (sh + 1)) == (q >> (sh + 1))) & ((r >> sh) != (q >> sh))).astype(np.float32))
        bs *= 2
    joins.append(1.0 - eye)
    return (jnp.asarray(mf), jnp.asarray(np.concatenate([incl, incl], axis=1), dtype=BF16),
            jnp.asarray(np.stack(joins), dtype=BF16))


def _gdn_kernel(x_ref, halo_ref, cw_ref, gate_ref, alog_ref, dtb_ref, z_ref, gn_ref, mf_ref, incl2_ref, mb_ref,
                o_ref, xs_ref, y_ref, s_ref, *, chunk, per_step):
    gi = pl.program_id(1)

    @pl.when(gi == 0)
    def _():
        s_ref[...] = jnp.zeros_like(s_ref)

    c = chunk
    dh = C_HEAD_DIM
    n_levels = mb_ref.shape[0] - 1

    xs_ref[0:V7X_SUBLANES, :] = jnp.where(gi > 0, halo_ref[...], 0.0)
    xs_ref[V7X_SUBLANES:, :] = x_ref[...]
    for c0 in range(0, 3 * C_WIDTH, CONV_COLS):
        cols = slice(c0, c0 + CONV_COLS)
        xs = xs_ref[:, cols]
        conv = cw_ref[CONV_K - 1:CONV_K, cols] * xs[V7X_SUBLANES:]
        for back in range(1, CONV_K):
            shifted = pltpu.roll(xs, back, axis=0)[V7X_SUBLANES:]
            conv = conv + cw_ref[CONV_K - 1 - back:CONV_K - back, cols] * shifted
        y_ref[:, cols] = _silu(conv)

    gates = gate_ref[...]
    beta_all = _sigmoid(gates)
    g_all = -jnp.exp(alog_ref[...]) * _softplus(gates + dtb_ref[...])
    scale = dh ** -0.5

    chains = [(ck, h) for ck in range(per_step) for h in range(C_HEADS)]
    qn, kn, kbeta, vbeta, gwide = {}, {}, {}, {}, {}
    for key in chains:
        ck, h = key
        rs = slice(ck * c, (ck + 1) * c)
        q = y_ref[rs, h * dh:(h + 1) * dh]
        k = y_ref[rs, C_WIDTH + h * dh:C_WIDTH + (h + 1) * dh]
        v = y_ref[rs, 2 * C_WIDTH + h * dh:2 * C_WIDTH + (h + 1) * dh]
        qn[key] = q * lax.rsqrt(jnp.sum(q * q, axis=-1, keepdims=True) + RMS_EPS)
        k = k * lax.rsqrt(jnp.sum(k * k, axis=-1, keepdims=True) + RMS_EPS)
        kn[key] = k
        beta_b = jnp.broadcast_to(beta_all[rs, h:h + 1], (c, dh))
        kbeta[key] = k * beta_b
        vbeta[key] = v * beta_b
        gwide[key] = jnp.broadcast_to(g_all[rs, C_HEADS + h:C_HEADS + h + 1], (c, c))

    gc = {key: _dot(incl2_ref[...], _split2_rows(gwide[key][:, :dh])) for key in chains}
    gdiff = {key: _dot(incl2_ref[...], _split2_rows(gwide[key] * mf_ref[MF_STRICT])) for key in chains}
    kf = {key: kn[key].astype(BF16) for key in chains}
    akk = {key: _dot_nt(kbeta[key].astype(BF16), kf[key]) for key in chains}
    aqk = {key: _dot_nt((qn[key] * scale).astype(BF16), kf[key]) for key in chains}
    nlm, xb, intra = {}, {}, {}
    for key in chains:
        e = jnp.exp(gdiff[key])
        nl = akk[key] * (e * mf_ref[MF_NSTRICT])
        nlm[key] = nl.astype(BF16)
        intra[key] = (aqk[key] * (e * mf_ref[MF_INCL])).astype(BF16)
        xb[key] = (nl * mf_ref[MF_PAIR] + mf_ref[MF_EYE]).astype(BF16)

    for lvl in range(n_levels):
        zb = {key: (_dot(xb[key], nlm[key] * mb_ref[lvl]) + mf_ref[MF_EYE]).astype(BF16) for key in chains}
        xb = {key: _dot(zb[key], xb[key]).astype(BF16) for key in chains}

    egc = {key: jnp.exp(gc[key]) for key in chains}
    rhs = {key: jnp.concatenate([vbeta[key], kbeta[key] * egc[key]], axis=1) for key in chains}
    sol = {key: rhs[key] + _dot(xb[key] * mb_ref[n_levels], rhs[key].astype(BF16)) for key in chains}
    wb = {key: sol[key][:, dh:].astype(BF16) for key in chains}
    qg = {key: (qn[key] * egc[key] * scale).astype(BF16) for key in chains}
    kdt, etot = {}, {}
    for key in chains:
        total = gc[key][c - 1:c, :]
        kdt[key] = (kn[key] * jnp.exp(total - gc[key])).T.astype(BF16)
        etot[key] = jnp.exp(total)

    gn = gn_ref[...]
    for ck in range(per_step):
        keys = [(ck, h) for h in range(C_HEADS)]
        rs = slice(ck * c, (ck + 1) * c)
        state = {key: s_ref[key[1]] for key in keys}
        sb = {key: state[key].astype(BF16) for key in keys}
        ws = {key: _dot(wb[key], sb[key]) for key in keys}
        qs = {key: _dot(qg[key], sb[key]) for key in keys}
        vnb = {key: (sol[key][:, :dh] - ws[key]).astype(BF16) for key in keys}
        o = {key: qs[key] + _dot(intra[key], vnb[key]) for key in keys}
        for key in keys:
            s_ref[key[1]] = state[key] * etot[key] + _dot(kdt[key], vnb[key])
        for key in keys:
            sl = slice(key[1] * dh, (key[1] + 1) * dh)
            on = o[key] * lax.rsqrt(jnp.mean(o[key] * o[key], axis=-1, keepdims=True) + RMS_EPS) * gn
            o_ref[rs, sl] = (on * _silu(z_ref[rs, sl])).astype(BF16)


def gdn(cbuf, gates, conv_w, alog_row, dtb_row, norm_g, batch, chunk=GDN_C, per_step=GDN_CHUNKS_PER_STEP):
    t = cbuf.shape[0]
    rows = chunk * per_step
    ng = t // batch // rows
    hb = rows // V7X_SUBLANES
    w3 = 3 * C_WIDTH
    mf, incl2, mb = _gdn_masks(chunk)
    kern = functools.partial(_gdn_kernel, chunk=chunk, per_step=per_step)
    row = lambda w: pl.BlockSpec((rows, w), lambda b, i: (b * ng + i, 0))
    const = lambda a: pl.BlockSpec(a.shape, lambda b, i: (0,) * a.ndim)
    return pl.pallas_call(
        kern,
        grid=(batch, ng),
        in_specs=[
            row(w3),
            pl.BlockSpec((V7X_SUBLANES, w3), lambda b, i: (jnp.maximum((b * ng + i) * hb - 1, 0), 0)),
            const(conv_w),
            row(V7X_LANES),
            pl.BlockSpec((1, V7X_LANES), lambda b, i: (0, 0)),
            pl.BlockSpec((1, V7X_LANES), lambda b, i: (0, 0)),
            pl.BlockSpec((rows, C_WIDTH), lambda b, i: (b * ng + i, 3)),
            pl.BlockSpec((1, C_HEAD_DIM), lambda b, i: (0, 0)),
            const(mf), const(incl2), const(mb),
        ],
        out_specs=row(C_WIDTH),
        out_shape=jax.ShapeDtypeStruct((t, C_WIDTH), BF16),
        scratch_shapes=[pltpu.VMEM((rows + V7X_SUBLANES, w3), F32),
                        pltpu.VMEM((rows, w3), F32),
                        pltpu.VMEM((C_HEADS, C_HEAD_DIM, C_HEAD_DIM), F32)],
        compiler_params=_cparams(("parallel", "arbitrary")),
        name="gdn",
    )(cbuf, cbuf, conv_w, gates, alog_row, dtb_row, cbuf, norm_g.reshape(1, C_HEAD_DIM), mf, incl2, mb)


def _out_kernel(x_ref, oa_ref, ob_ref, oc_ref, w_ref, g_ref, b_ref, o_ref):
    mix = _dot(oa_ref[...], w_ref[0:A_WIDTH, :])
    mix = mix + _dot(ob_ref[...], w_ref[A_WIDTH:A_WIDTH + B_WIDTH, :])
    mix = mix + _dot(oc_ref[...], w_ref[A_WIDTH + B_WIDTH:, :])
    y = DEEPNORM_ALPHA * x_ref[...] + mix
    o_ref[...] = _layer_norm(y, g_ref[...], b_ref[...])


def out_ln(x, oa, ob, oc, w_out, g, b, tm=OUT_TM):
    t, d = x.shape
    tm = min(tm, t)
    row = lambda w: pl.BlockSpec((tm, w), lambda i: (i, 0))
    return pl.pallas_call(
        _out_kernel,
        grid=(t // tm,),
        in_specs=[row(d), row(A_WIDTH), row(B_WIDTH), row(C_WIDTH),
                  pl.BlockSpec(w_out.shape, lambda i: (0, 0)),
                  pl.BlockSpec((1, d), lambda i: (0, 0)), pl.BlockSpec((1, d), lambda i: (0, 0))],
        out_specs=row(d),
        out_shape=jax.ShapeDtypeStruct((t, d), F32),
        compiler_params=_cparams(("parallel",)),
        name="out_ln",
    )(x, oa, ob, oc, w_out, g, b)


def _split_w_in(w):
    o = 0
    parts = {}
    for name, n in (("qa", 256), ("ka", 256), ("va", 256), ("qb", 256), ("kb", 256), ("vb", 256),
                    ("qkv", 3 * C_WIDTH), ("z", C_WIDTH), ("beta", C_HEADS), ("a", C_HEADS)):
        parts[name] = w[:, o:o + n]
        o += n
    wk = jnp.concatenate([parts["ka"], parts["kb"]], axis=1).astype(BF16)
    wt = jnp.concatenate([parts["qa"], parts["va"], parts["qb"], parts["vb"]], axis=1).T.astype(BF16)
    wc = jnp.concatenate([parts["qkv"], parts["z"]], axis=1).astype(BF16)
    pad = jnp.zeros((w.shape[0], V7X_LANES - 2 * C_HEADS), w.dtype)
    wg = jnp.concatenate([parts["beta"], parts["a"], pad], axis=1).astype(BF16)
    return wk, wt, wc, wg


def _gate_row(vals):
    row = jnp.zeros((1, V7X_LANES), F32)
    return row.at[0, C_HEADS:2 * C_HEADS].set(vals.astype(F32))


def kernel(x, ffn1_w_gu, ffn1_w_down, ffn2_w_gu, ffn2_w_down, ln_g, ln_b, w_in, conv_w, dn_a_log, dn_dt_bias,
           dn_norm_g, diff_lambda, diff_norm_g, sb_norm_g, w_out):
    batch, seq, d = x.shape
    h = x.reshape(batch * seq, d)
    for l in range(DEPTH):
        lambda_init = 0.8 - 0.6 * math.exp(-0.3 * l)
        ln = lambda i: (ln_g[l, i].reshape(1, d), ln_b[l, i].reshape(1, d))
        h = ffn_ln(h, ffn1_w_gu, ffn1_w_down, l, *ln(0))
        wk, wt, wc, wg = _split_w_in(w_in[l])
        ka, kb, qta, vta, qtb, vtb, cbuf, gates = in_proj(h, wk, wt, wc, wg)
        oa = diff_attn(qta, ka, vta, diff_lambda[l], diff_norm_g[l], batch, lambda_init)
        ob = stick_attn(qtb, kb, vtb, sb_norm_g[l], batch)
        oc = gdn(cbuf, gates, conv_w[l], _gate_row(dn_a_log[l]), _gate_row(dn_dt_bias[l]), dn_norm_g[l], batch)
        h = out_ln(h, oa, ob, oc, w_out[l].astype(BF16), *ln(1))
        h = ffn_ln(h, ffn2_w_gu, ffn2_w_down, l, *ln(2))
    return h.reshape(batch, seq, d)
```

```python
import functools
import math

import jax
import jax.numpy as jnp
import numpy as np
from jax import lax
from jax.experimental import pallas as pl
from jax.experimental.pallas import tpu as pltpu

F32 = jnp.float32
BF16 = jnp.bfloat16

DEPTH = 2
A_HEADS, A_QK_DIM, A_V_DIM = 4, 32, 64
B_HEADS, B_HEAD_DIM = 4, 64
C_HEADS, C_HEAD_DIM = 4, 128
CONV_K = 4
D_FF = 2816
A_WIDTH, B_WIDTH, C_WIDTH = 256, 256, 512
DEEPNORM_ALPHA = (2.0 * DEPTH) ** 0.25
LN_EPS = 1e-5
RMS_EPS = 1e-6
NEG_BIG = -1e30
LOG2_E = math.log2(math.e)
STICK_DEAD_LOG = 104.0

V7X_LANES = 128
V7X_SUBLANES = 8
V7X_VMEM_LIMIT_BYTES = 56 * 1024 * 1024
ONES_ROWS = 16

FFN_TM = 2048
FFN_TF = 256
FFN_LN_ROWS = 256
PROJ_TM = 512
ATT_T = 256
GDN_C = 256
OUT_TM = 1024
OUT_LN_ROWS = 256
CONV_COLS = 512
GDN_CHUNKS_PER_STEP = 2


def _cparams(sem):
    return pltpu.CompilerParams(dimension_semantics=sem, vmem_limit_bytes=V7X_VMEM_LIMIT_BYTES)


def _layer_norm(y, g, b):
    mu = jnp.mean(y, axis=-1, keepdims=True)
    d = y - mu
    var = jnp.mean(d * d, axis=-1, keepdims=True)
    return d * lax.rsqrt(var + LN_EPS) * g + b


def _sigmoid(x):
    return 0.5 * jnp.tanh(0.5 * x) + 0.5


def _silu(x):
    return x * _sigmoid(x)


def _softplus(x):
    return jnp.maximum(x, 0.0) + jnp.log1p(jnp.exp(-jnp.abs(x)))


def _dot(a, b):
    return jnp.dot(a, b, preferred_element_type=F32)


def _dot_nt(a, b):
    return lax.dot_general(a, b, (((1,), (1,)), ((), ())), preferred_element_type=F32)


def _split2_rows(x):
    hi = x.astype(BF16)
    lo = (x - hi.astype(F32)).astype(BF16)
    return jnp.concatenate([hi, lo], axis=0)


def _ffn_kernel(x_ref, wg_ref, wu_ref, wd_ref, g_ref, b_ref, o_ref, xb_ref):
    j = pl.program_id(1)
    last = pl.num_programs(1) - 1
    tm = x_ref.shape[0]

    def hidden_chunk(xb, weights=None):
        wg, wu, wd = weights or (wg_ref[...].astype(BF16), wu_ref[...].astype(BF16), wd_ref[...].astype(BF16))
        gate = _dot(xb, wg)
        up = _dot(xb, wu)
        act = (_silu(gate) * up).astype(BF16)
        return _dot(act, wd)

    @pl.when(j == 0)
    def _():
        xb = x_ref[...].astype(BF16)
        xb_ref[...] = xb
        o_ref[...] = hidden_chunk(xb)

    @pl.when((j > 0) & (j < last))
    def _():
        o_ref[...] += hidden_chunk(xb_ref[...])

    @pl.when((j > 0) & (j == last))
    def _():
        weights = (wg_ref[...].astype(BF16), wu_ref[...].astype(BF16), wd_ref[...].astype(BF16))
        for r0 in range(0, tm, FFN_LN_ROWS):
            rows = slice(r0, min(r0 + FFN_LN_ROWS, tm))
            acc = o_ref[rows, :] + hidden_chunk(xb_ref[rows, :], weights)
            y = DEEPNORM_ALPHA * x_ref[rows, :] + 0.5 * acc
            o_ref[rows, :] = _layer_norm(y, g_ref[...], b_ref[...])


def ffn_ln(x, w_gu, w_down, layer, g, b, tm=FFN_TM, tf=FFN_TF):
    t, d = x.shape
    ff = w_down.shape[1]
    tm = min(tm, t)
    nf = ff // tf
    assert nf > 1 and ff % tf == 0 and t % tm == 0
    return pl.pallas_call(
        _ffn_kernel,
        grid=(t // tm, nf),
        in_specs=[
            pl.BlockSpec((tm, d), lambda i, j: (i, 0)),
            pl.BlockSpec((None, d, tf), lambda i, j: (layer, 0, j)),
            pl.BlockSpec((None, d, tf), lambda i, j: (layer, 0, j + nf)),
            pl.BlockSpec((None, tf, d), lambda i, j: (layer, j, 0)),
            pl.BlockSpec((1, d), lambda i, j: (0, 0)),
            pl.BlockSpec((1, d), lambda i, j: (0, 0)),
        ],
        out_specs=pl.BlockSpec((tm, d), lambda i, j: (i, 0)),
        out_shape=jax.ShapeDtypeStruct((t, d), F32),
        scratch_shapes=[pltpu.VMEM((tm, d), BF16)],
        compiler_params=_cparams(("parallel", "arbitrary")),
        name="ffn_ln",
    )(x, w_gu, w_gu, w_down, g, b)


def _proj_kernel(x_ref, wk_ref, wt_ref, wc_ref, wg_ref,
                 ka_ref, kb_ref, qta_ref, vta_ref, qtb_ref, vtb_ref, c_ref, gate_ref):
    xb = x_ref[...].astype(BF16)
    c_ref[...] = _dot(xb, wc_ref[...])
    k = _dot(xb, wk_ref[...])
    ka_ref[...] = k[:, :A_WIDTH].astype(BF16)
    kb_ref[...] = k[:, A_WIDTH:].astype(BF16)
    t = _dot_nt(wt_ref[...], xb)
    for s in range(qta_ref.shape[0]):
        cs = slice(s * ATT_T, (s + 1) * ATT_T)
        qta_ref[s] = (t[0:256, cs] * (A_QK_DIM ** -0.5 * LOG2_E)).astype(BF16)
        vta_ref[s] = t[256:512, cs].astype(BF16)
        qtb_ref[s] = (t[512:768, cs] * (B_HEAD_DIM ** -0.5)).astype(BF16)
        vtb_ref[s] = t[768:1024, cs].astype(BF16)
    gate_ref[...] = _dot(xb, wg_ref[...])


def in_proj(x, wk, wt, wc, wg, tm=PROJ_TM):
    t, d = x.shape
    tm = min(tm, t)
    nt = t // tm
    full = lambda a: pl.BlockSpec(a.shape, lambda i: (0,) * a.ndim)
    row = lambda w: pl.BlockSpec((tm, w), lambda i: (i, 0))
    tr = pl.BlockSpec((tm // ATT_T, 256, ATT_T), lambda i: (i, 0, 0))
    tr_shape = jax.ShapeDtypeStruct((t // ATT_T, 256, ATT_T), BF16)
    return pl.pallas_call(
        _proj_kernel,
        grid=(nt,),
        in_specs=[row(d), full(wk), full(wt), full(wc), full(wg)],
        out_specs=[row(256), row(256), tr, tr, tr, tr, row(wc.shape[1]), row(V7X_LANES)],
        out_shape=[
            jax.ShapeDtypeStruct((t, 256), BF16), jax.ShapeDtypeStruct((t, 256), BF16),
            tr_shape, tr_shape, tr_shape, tr_shape,
            jax.ShapeDtypeStruct((t, wc.shape[1]), F32), jax.ShapeDtypeStruct((t, V7X_LANES), F32),
        ],
        compiler_params=_cparams(("parallel",)),
        name="in_proj",
    )(x, wk, wt, wc, wg)


def _diff_attn_kernel(q_ref, k_ref, v_ref, lam_ref, g_ref, o_ref,
                      km_ref, m_ref, a_ref, acc_ref, s_ref, tm_ref, p_ref, ot_ref, *, lambda_init, tile, nq):
    qi = pl.program_id(1)
    nc = 2 * A_HEADS
    qt = q_ref[0]

    @pl.when(qi == 0)
    def _():
        lane = lax.broadcasted_iota(jnp.int32, (tile, 256), 1)
        for j in range(nq):
            kj = k_ref[j * tile:(j + 1) * tile, :]
            for c in range(nc):
                keep = (lane >= c * A_QK_DIM) & (lane < (c + 1) * A_QK_DIM)
                km_ref[j, c * tile:(c + 1) * tile, :] = jnp.where(keep, kj, jnp.zeros_like(kj))

    def scores(j, slot):
        s = _dot(km_ref[j], qt)
        s_ref[slot] = s
        for c in range(nc):
            tm_ref[slot, c] = jnp.max(s[c * tile:(c + 1) * tile], axis=0, keepdims=True)

    m_ref[...] = jnp.full(m_ref.shape, NEG_BIG, F32)
    acc_ref[...] = jnp.zeros_like(acc_ref)
    ones = jnp.ones((ONES_ROWS, tile), BF16)

    def step(cur, src, nxt, dst, masked):
        vt = v_ref[cur]
        if masked:
            key = lax.broadcasted_iota(jnp.int32, (tile, tile), 0)
            qry = lax.broadcasted_iota(jnp.int32, (tile, tile), 1)
            causal = key <= qry
        if nxt is not None:
            scores(nxt, dst)
        m_new = []
        for c in range(nc):
            if masked:
                tmax = jnp.max(jnp.where(causal, s_ref[src, c * tile:(c + 1) * tile, :], NEG_BIG),
                               axis=0, keepdims=True)
            else:
                tmax = tm_ref[src, c]
            m_old = m_ref[c]
            m_new.append(jnp.maximum(m_old, tmax))
            a_ref[c] = jnp.exp2(m_old - m_new[c])
            m_ref[c] = m_new[c]

        def pv(c):
            h = c // 2
            vt_ext = jnp.concatenate([vt[h * A_V_DIM:(h + 1) * A_V_DIM, :], ones], axis=0)
            acc_ref[c] = a_ref[c] * acc_ref[c] + _dot(vt_ext, p_ref[c])

        for c in range(nc):
            s = s_ref[src, c * tile:(c + 1) * tile, :]
            if masked:
                s = jnp.where(causal, s, NEG_BIG)
            p_ref[c] = jnp.exp2((s - m_new[c]).astype(BF16))
            if c > 0:
                pv(c - 1)
        pv(nc - 1)

    scores(0, 0)

    def body(jj, carry):
        j = 2 * jj
        step(j, 0, j + 1, 1, False)
        step(j + 1, 1, j + 2, 0, False)
        return carry

    lax.fori_loop(0, qi // 2, body, 0)

    @pl.when(qi % 2 == 0)
    def _():
        step(qi, 0, None, None, True)

    @pl.when(qi % 2 == 1)
    def _():
        step(qi - 1, 0, qi, 1, False)
        step(qi, 1, None, None, True)

    lf = lam_ref[...]
    lam = (jnp.exp(jnp.sum(lf[0:1] * lf[1:2], axis=-1, keepdims=True))
           - jnp.exp(jnp.sum(lf[2:3] * lf[3:4], axis=-1, keepdims=True)) + lambda_init)
    g = g_ref[...]
    dv = A_V_DIM
    for h in range(A_HEADS):
        a1 = acc_ref[2 * h]
        a2 = acc_ref[2 * h + 1]
        o = a1[:dv] / a1[dv:dv + 1] - lam * (a2[:dv] / a2[dv:dv + 1])
        ms = jnp.mean(o * o, axis=0, keepdims=True)
        ot_ref[h * A_V_DIM:(h + 1) * A_V_DIM, :] = o * lax.rsqrt(ms + RMS_EPS) * g * (1.0 - lambda_init)
    o_ref[...] = ot_ref[...].T.astype(BF16)


def diff_attn(qt, k, vt, diff_lambda, norm_g, batch, lambda_init, tile=ATT_T):
    t = k.shape[0]
    s = t // batch
    nq = s // tile
    nc = 2 * A_HEADS
    kern = functools.partial(_diff_attn_kernel, lambda_init=lambda_init, tile=tile, nq=nq)
    return pl.pallas_call(
        kern,
        grid=(batch, nq),
        in_specs=[
            pl.BlockSpec((1, 256, tile), lambda b, i: (b * nq + i, 0, 0)),
            pl.BlockSpec((s, 256), lambda b, i: (b, 0)),
            pl.BlockSpec((nq, 256, tile), lambda b, i: (b, 0, 0)),
            pl.BlockSpec(diff_lambda.shape, lambda b, i: (0, 0)),
            pl.BlockSpec((A_V_DIM, 1), lambda b, i: (0, 0)),
        ],
        out_specs=pl.BlockSpec((tile, 256), lambda b, i: (b * nq + i, 0)),
        out_shape=jax.ShapeDtypeStruct((t, 256), BF16),
        scratch_shapes=[
            pltpu.VMEM((nq, nc * tile, 256), BF16),
            pltpu.VMEM((nc, 1, tile), F32),
            pltpu.VMEM((nc, 1, tile), F32),
            pltpu.VMEM((nc, A_V_DIM + ONES_ROWS, tile), F32),
            pltpu.VMEM((2, nc * tile, tile), F32),
            pltpu.VMEM((2, nc, 1, tile), F32),
            pltpu.VMEM((nc, tile, tile), BF16),
            pltpu.VMEM((256, tile), F32),
        ],
        compiler_params=_cparams(("parallel", "arbitrary")),
        name="diff_attn",
    )(qt, k, vt, diff_lambda, norm_g.reshape(A_V_DIM, 1))


def _stick_attn_kernel(q_ref, k_ref, v_ref, g_ref, o_ref,
                       qm_ref, carry_ref, acc_ref, z_ref, hl_ref, t_ref, w_ref, ot_ref, *, tile):
    qi = pl.program_id(1)
    qt = q_ref[0]
    row = lax.broadcasted_iota(jnp.int32, qt.shape, 0)
    for h in range(B_HEADS):
        keep = (row >= h * B_HEAD_DIM) & (row < (h + 1) * B_HEAD_DIM)
        qm_ref[h] = jnp.where(keep, qt, jnp.zeros_like(qt))
    carry_ref[...] = jnp.zeros_like(carry_ref)
    acc_ref[...] = jnp.zeros_like(acc_ref)

    key = lax.broadcasted_iota(jnp.int32, (tile, tile), 0)
    qry = lax.broadcasted_iota(jnp.int32, (tile, tile), 1)
    upper = (qry > key).astype(BF16)
    upper2 = jnp.concatenate([upper, upper], axis=1)
    strict = key < qry

    def do_tiles(tiles):
        slots = range(len(tiles))
        for ti in slots:
            j = tiles[ti][0]
            kb = k_ref[pl.ds(pl.multiple_of(j * tile, tile), tile), :]
            for h in range(B_HEADS):
                z_ref[ti, h] = _dot(kb, qm_ref[h])
        csum = {}
        for ti in slots:
            for h in range(B_HEADS):
                z = z_ref[ti, h]
                sp = jnp.maximum(z, 0.0) + jnp.log(1.0 + jnp.exp(-jnp.abs(z)))
                if tiles[ti][1]:
                    sp = jnp.where(strict, sp, 0.0)
                hi = sp.astype(BF16)
                hl_ref[ti, h, 0:tile, :] = hi
                hl_ref[ti, h, tile:, :] = (sp - hi.astype(F32)).astype(BF16)
                z_ref[ti, h] = z - sp
                csum[ti, h] = jnp.sum(sp, axis=0, keepdims=True)
        before = {}
        for h in range(B_HEADS):
            carry = carry_ref[h]
            for ti in slots:
                before[ti, h] = carry
                carry = carry + csum[ti, h]
            carry_ref[h] = carry
        for ti in slots:
            for h in range(B_HEADS):
                t_ref[ti, h] = _dot(upper2, hl_ref[ti, h])
        for ti in slots:
            for h in range(B_HEADS):
                w = jnp.exp(z_ref[ti, h] - t_ref[ti, h] - before[ti, h])
                if tiles[ti][1]:
                    w = jnp.where(strict, w, 0.0)
                w_ref[ti, h] = w.astype(BF16)
        for ti in slots:
            vt = v_ref[tiles[ti][0]]
            for h in range(B_HEADS):
                acc_ref[h] += _dot(vt[h * B_HEAD_DIM:(h + 1) * B_HEAD_DIM, :], w_ref[ti, h])

    @pl.when(qi == 0)
    def _():
        do_tiles([(qi, True)])

    @pl.when(qi > 0)
    def _():
        do_tiles([(qi, True), (qi - 1, False)])

    def live():
        return jnp.min(carry_ref[...]) < STICK_DEAD_LOG

    def cond(state):
        jj, go = state
        return jnp.logical_and(jj < qi, go)

    def body(state):
        jj, _ = state
        do_tiles([(qi - 1 - jj, False)])
        return jj + 1, live()

    lax.while_loop(cond, body, (jnp.int32(1), live()))

    g = g_ref[...]
    for h in range(B_HEADS):
        o = acc_ref[h]
        ms = jnp.mean(o * o, axis=0, keepdims=True)
        ot_ref[h * B_HEAD_DIM:(h + 1) * B_HEAD_DIM, :] = o * lax.rsqrt(ms + RMS_EPS) * g
    o_ref[...] = ot_ref[...].T.astype(BF16)


def stick_attn(qt, k, vt, norm_g, batch, tile=ATT_T):
    t = k.shape[0]
    s = t // batch
    nq = s // tile
    kern = functools.partial(_stick_attn_kernel, tile=tile)
    return pl.pallas_call(
        kern,
        grid=(batch, nq),
        in_specs=[
            pl.BlockSpec((1, 256, tile), lambda b, i: (b * nq + i, 0, 0)),
            pl.BlockSpec((s, 256), lambda b, i: (b, 0)),
            pl.BlockSpec((nq, 256, tile), lambda b, i: (b, 0, 0)),
            pl.BlockSpec((B_HEAD_DIM, 1), lambda b, i: (0, 0)),
        ],
        out_specs=pl.BlockSpec((tile, 256), lambda b, i: (b * nq + i, 0)),
        out_shape=jax.ShapeDtypeStruct((t, 256), BF16),
        scratch_shapes=[
            pltpu.VMEM((B_HEADS, 256, tile), BF16),
            pltpu.VMEM((B_HEADS, 1, tile), F32),
            pltpu.VMEM((B_HEADS, B_HEAD_DIM, tile), F32),
            pltpu.VMEM((2, B_HEADS, tile, tile), F32),
            pltpu.VMEM((2, B_HEADS, 2 * tile, tile), BF16),
            pltpu.VMEM((2, B_HEADS, tile, tile), F32),
            pltpu.VMEM((2, B_HEADS, tile, tile), BF16),
            pltpu.VMEM((256, tile), F32),
        ],
        compiler_params=_cparams(("parallel", "arbitrary")),
        name="stick_attn",
    )(qt, k, vt, norm_g.reshape(B_HEAD_DIM, 1))


MF_STRICT, MF_NSTRICT, MF_INCL, MF_EYE, MF_PAIR = range(5)


def _gdn_masks(c):
    r = np.arange(c)[:, None]
    q = np.arange(c)[None, :]
    strict = (q < r).astype(np.float32)
    incl = (q <= r).astype(np.float32)
    eye = (q == r).astype(np.float32)
    pair = ((r >> 1) == (q >> 1)).astype(np.float32)
    mf = np.stack([strict, -strict, incl, eye, pair])
    joins = []
    bs = 2
    while bs < c:
        sh = bs.bit_length() - 1
        joins.append((((r >> (sh + 1)) == (q >> (sh + 1))) & ((r >> sh) != (q >> sh))).astype(np.float32))
        bs *= 2
    joins.append(1.0 - eye)
    return (jnp.asarray(mf), jnp.asarray(np.concatenate([incl, incl], axis=1), dtype=BF16),
            jnp.asarray(np.stack(joins), dtype=BF16))


def _gdn_kernel(x_ref, halo_ref, cw_ref, gate_ref, alog_ref, dtb_ref, z_ref, gn_ref, mf_ref, incl2_ref, mb_ref,
                o_ref, xs_ref, y_ref, s_ref, *, chunk, per_step):
    gi = pl.program_id(1)

    @pl.when(gi == 0)
    def _():
        s_ref[...] = jnp.zeros_like(s_ref)

    c = chunk
    dh = C_HEAD_DIM
    n_levels = mb_ref.shape[0] - 1

    xs_ref[0:V7X_SUBLANES, :] = jnp.where(gi > 0, halo_ref[...], 0.0)
    xs_ref[V7X_SUBLANES:, :] = x_ref[...]
    for c0 in range(0, 3 * C_WIDTH, CONV_COLS):
        cols = slice(c0, c0 + CONV_COLS)
        xs = xs_ref[:, cols]
        conv = cw_ref[CONV_K - 1:CONV_K, cols] * xs[V7X_SUBLANES:]
        for back in range(1, CONV_K):
            shifted = pltpu.roll(xs, back, axis=0)[V7X_SUBLANES:]
            conv = conv + cw_ref[CONV_K - 1 - back:CONV_K - back, cols] * shifted
        y_ref[:, cols] = _silu(conv)

    gates = gate_ref[...]
    beta_all = _sigmoid(gates)
    g_all = -jnp.exp(alog_ref[...]) * _softplus(gates + dtb_ref[...])
    scale = dh ** -0.5

    chains = [(ck, h) for ck in range(per_step) for h in range(C_HEADS)]
    qn, kn, kbeta, vbeta, gwide = {}, {}, {}, {}, {}
    for key in chains:
        ck, h = key
        rs = slice(ck * c, (ck + 1) * c)
        q = y_ref[rs, h * dh:(h + 1) * dh]
        k = y_ref[rs, C_WIDTH + h * dh:C_WIDTH + (h + 1) * dh]
        v = y_ref[rs, 2 * C_WIDTH + h * dh:2 * C_WIDTH + (h + 1) * dh]
        qn[key] = q * lax.rsqrt(jnp.sum(q * q, axis=-1, keepdims=True) + RMS_EPS)
        k = k * lax.rsqrt(jnp.sum(k * k, axis=-1, keepdims=True) + RMS_EPS)
        kn[key] = k
        beta_b = jnp.broadcast_to(beta_all[rs, h:h + 1], (c, dh))
        kbeta[key] = k * beta_b
        vbeta[key] = v * beta_b
        gwide[key] = jnp.broadcast_to(g_all[rs, C_HEADS + h:C_HEADS + h + 1], (c, c))

    gdiff = {key: _dot(incl2_ref[...], _split2_rows(gwide[key] * mf_ref[MF_STRICT])) for key in chains}
    gc = {key: jnp.broadcast_to(gdiff[key][:, 0:1] + gwide[key][0:1, 0:1], (c, dh)) for key in chains}
    kf = {key: kn[key].astype(BF16) for key in chains}
    akk = {key: _dot_nt(kbeta[key].astype(BF16), kf[key]) for key in chains}
    aqk = {key: _dot_nt((qn[key] * scale).astype(BF16), kf[key]) for key in chains}
    nlm, xb, intra = {}, {}, {}
    for key in chains:
        e = jnp.exp(gdiff[key])
        nl = akk[key] * (e * mf_ref[MF_NSTRICT])
        nlm[key] = nl.astype(BF16)
        intra[key] = (aqk[key] * (e * mf_ref[MF_INCL])).astype(BF16)
        xb[key] = (nl * mf_ref[MF_PAIR] + mf_ref[MF_EYE]).astype(BF16)

    for lvl in range(n_levels):
        zb = {key: (_dot(xb[key], nlm[key] * mb_ref[lvl]) + mf_ref[MF_EYE]).astype(BF16) for key in chains}
        xb = {key: _dot(zb[key], xb[key]).astype(BF16) for key in chains}

    egc = {key: jnp.exp(gc[key]) for key in chains}
    rhs = {key: jnp.concatenate([vbeta[key], kbeta[key] * egc[key]], axis=1) for key in chains}
    sol = {key: rhs[key] + _dot(xb[key] * mb_ref[n_levels], rhs[key].astype(BF16)) for key in chains}
    wb = {key: sol[key][:, dh:].astype(BF16) for key in chains}
    qg = {key: (qn[key] * egc[key] * scale).astype(BF16) for key in chains}
    kdt, etot = {}, {}
    for key in chains:
        total = gc[key][c - 1:c, :]
        kdt[key] = (kn[key] * jnp.exp(total - gc[key])).T.astype(BF16)
        etot[key] = jnp.exp(total)

    gn = gn_ref[...]
    for ck in range(per_step):
        keys = [(ck, h) for h in range(C_HEADS)]
        rs = slice(ck * c, (ck + 1) * c)
        state = {key: s_ref[key[1]] for key in keys}
        sb = {key: state[key].astype(BF16) for key in keys}
        ws = {key: _dot(wb[key], sb[key]) for key in keys}
        qs = {key: _dot(qg[key], sb[key]) for key in keys}
        vnb = {key: (sol[key][:, :dh] - ws[key]).astype(BF16) for key in keys}
        o = {key: qs[key] + _dot(intra[key], vnb[key]) for key in keys}
        for key in keys:
            s_ref[key[1]] = state[key] * etot[key] + _dot(kdt[key], vnb[key])
        for key in keys:
            sl = slice(key[1] * dh, (key[1] + 1) * dh)
            on = o[key] * lax.rsqrt(jnp.mean(o[key] * o[key], axis=-1, keepdims=True) + RMS_EPS) * gn
            o_ref[rs, sl] = (on * _silu(z_ref[rs, sl])).astype(BF16)


def gdn(cbuf, gates, conv_w, alog_row, dtb_row, norm_g, batch, chunk=GDN_C, per_step=GDN_CHUNKS_PER_STEP):
    t = cbuf.shape[0]
    rows = chunk * per_step
    ng = t // batch // rows
    hb = rows // V7X_SUBLANES
    w3 = 3 * C_WIDTH
    mf, incl2, mb = _gdn_masks(chunk)
    kern = functools.partial(_gdn_kernel, chunk=chunk, per_step=per_step)
    row = lambda w: pl.BlockSpec((rows, w), lambda b, i: (b * ng + i, 0))
    const = lambda a: pl.BlockSpec(a.shape, lambda b, i: (0,) * a.ndim)
    return pl.pallas_call(
        kern,
        grid=(batch, ng),
        in_specs=[
            row(w3),
            pl.BlockSpec((V7X_SUBLANES, w3), lambda b, i: (jnp.maximum((b * ng + i) * hb - 1, 0), 0)),
            const(conv_w),
            row(V7X_LANES),
            pl.BlockSpec((1, V7X_LANES), lambda b, i: (0, 0)),
            pl.BlockSpec((1, V7X_LANES), lambda b, i: (0, 0)),
            pl.BlockSpec((rows, C_WIDTH), lambda b, i: (b * ng + i, 3)),
            pl.BlockSpec((1, C_HEAD_DIM), lambda b, i: (0, 0)),
            const(mf), const(incl2), const(mb),
        ],
        out_specs=row(C_WIDTH),
        out_shape=jax.ShapeDtypeStruct((t, C_WIDTH), BF16),
        scratch_shapes=[pltpu.VMEM((rows + V7X_SUBLANES, w3), F32),
                        pltpu.VMEM((rows, w3), F32),
                        pltpu.VMEM((C_HEADS, C_HEAD_DIM, C_HEAD_DIM), F32)],
        compiler_params=_cparams(("parallel", "arbitrary")),
        name="gdn",
    )(cbuf, cbuf, conv_w, gates, alog_row, dtb_row, cbuf, norm_g.reshape(1, C_HEAD_DIM), mf, incl2, mb)


def _out_kernel(x_ref, oa_ref, ob_ref, oc_ref, w_ref, g_ref, b_ref, o_ref):
    tm = x_ref.shape[0]
    for r0 in range(0, tm, OUT_LN_ROWS):
        rows = slice(r0, min(r0 + OUT_LN_ROWS, tm))
        mix = _dot(oa_ref[rows, :], w_ref[0:A_WIDTH, :])
        mix = mix + _dot(ob_ref[rows, :], w_ref[A_WIDTH:A_WIDTH + B_WIDTH, :])
        mix = mix + _dot(oc_ref[rows, :], w_ref[A_WIDTH + B_WIDTH:, :])
        y = DEEPNORM_ALPHA * x_ref[rows, :] + mix
        o_ref[rows, :] = _layer_norm(y, g_ref[...], b_ref[...])


def out_ln(x, oa, ob, oc, w_out, g, b, tm=OUT_TM):
    t, d = x.shape
    tm = min(tm, t)
    row = lambda w: pl.BlockSpec((tm, w), lambda i: (i, 0))
    return pl.pallas_call(
        _out_kernel,
        grid=(t // tm,),
        in_specs=[row(d), row(A_WIDTH), row(B_WIDTH), row(C_WIDTH),
                  pl.BlockSpec(w_out.shape, lambda i: (0, 0)),
                  pl.BlockSpec((1, d), lambda i: (0, 0)), pl.BlockSpec((1, d), lambda i: (0, 0))],
        out_specs=row(d),
        out_shape=jax.ShapeDtypeStruct((t, d), F32),
        compiler_params=_cparams(("parallel",)),
        name="out_ln",
    )(x, oa, ob, oc, w_out, g, b)


def _split_w_in(w):
    o = 0
    parts = {}
    for name, n in (("qa", 256), ("ka", 256), ("va", 256), ("qb", 256), ("kb", 256), ("vb", 256),
                    ("qkv", 3 * C_WIDTH), ("z", C_WIDTH), ("beta", C_HEADS), ("a", C_HEADS)):
        parts[name] = w[:, o:o + n]
        o += n
    wk = jnp.concatenate([parts["ka"], parts["kb"]], axis=1).astype(BF16)
    wt = jnp.concatenate([parts["qa"], parts["va"], parts["qb"], parts["vb"]], axis=1).T.astype(BF16)
    wc = jnp.concatenate([parts["qkv"], parts["z"]], axis=1).astype(BF16)
    pad = jnp.zeros((w.shape[0], V7X_LANES - 2 * C_HEADS), w.dtype)
    wg = jnp.concatenate([parts["beta"], parts["a"], pad], axis=1).astype(BF16)
    return wk, wt, wc, wg


def _gate_row(vals):
    row = jnp.zeros((1, V7X_LANES), F32)
    return row.at[0, C_HEADS:2 * C_HEADS].set(vals.astype(F32))


def kernel(x, ffn1_w_gu, ffn1_w_down, ffn2_w_gu, ffn2_w_down, ln_g, ln_b, w_in, conv_w, dn_a_log, dn_dt_bias,
           dn_norm_g, diff_lambda, diff_norm_g, sb_norm_g, w_out):
    batch, seq, d = x.shape
    h = x.reshape(batch * seq, d)
    for l in range(DEPTH):
        lambda_init = 0.8 - 0.6 * math.exp(-0.3 * l)
        ln = lambda i: (ln_g[l, i].reshape(1, d), ln_b[l, i].reshape(1, d))
        h = ffn_ln(h, ffn1_w_gu, ffn1_w_down, l, *ln(0))
        wk, wt, wc, wg = _split_w_in(w_in[l])
        ka, kb, qta, vta, qtb, vtb, cbuf, gates = in_proj(h, wk, wt, wc, wg)
        oa = diff_attn(qta, ka, vta, diff_lambda[l], diff_norm_g[l], batch, lambda_init)
        ob = stick_attn(qtb, kb, vtb, sb_norm_g[l], batch)
        oc = gdn(cbuf, gates, conv_w[l], _gate_row(dn_a_log[l]), _gate_row(dn_dt_bias[l]), dn_norm_g[l], batch)
        h = out_ln(h, oa, ob, oc, w_out[l].astype(BF16), *ln(1))
        h = ffn_ln(h, ffn2_w_gu, ffn2_w_down, l, *ln(2))
    return h.reshape(batch, seq, d)
```

```python
import functools
import math

import jax
import jax.numpy as jnp
import numpy as np
from jax import lax
from jax.experimental import pallas as pl
from jax.experimental.pallas import tpu as pltpu

F32 = jnp.float32
BF16 = jnp.bfloat16

DEPTH = 2
A_HEADS, A_QK_DIM, A_V_DIM = 4, 32, 64
B_HEADS, B_HEAD_DIM = 4, 64
C_HEADS, C_HEAD_DIM = 4, 128
CONV_K = 4
A_WIDTH, B_WIDTH, C_WIDTH = 256, 256, 512
GROUP_W = 256
DEEPNORM_ALPHA = (2.0 * DEPTH) ** 0.25
LN_EPS = 1e-5
RMS_EPS = 1e-6
NEG_BIG = -1e30
LOG2_E = math.log2(math.e)
STICK_DEAD_LOG = 104.0

V7X_LANES = 128
V7X_SUBLANES = 8
V7X_VMEM_LIMIT_BYTES = 56 * 1024 * 1024
ONES_ROWS = 16

FFN_TM = 2048
FFN_TF = 256
FFN_LN_ROWS = 256
PROJ_TM = 1024
ATT_T = 256
GDN_C = 256
OUT_TM = 1024
OUT_LN_ROWS = 256
CONV_COLS = 512
GDN_CHUNKS_PER_STEP = 2


def _cparams(sem):
    return pltpu.CompilerParams(dimension_semantics=sem, vmem_limit_bytes=V7X_VMEM_LIMIT_BYTES)


def _layer_norm(y, g, b):
    mu = jnp.mean(y, axis=-1, keepdims=True)
    d = y - mu
    var = jnp.mean(d * d, axis=-1, keepdims=True)
    return d * lax.rsqrt(var + LN_EPS) * g + b


def _sigmoid(x):
    return 0.5 * jnp.tanh(0.5 * x) + 0.5


def _silu(x):
    return x * _sigmoid(x)


def _softplus(x):
    return jnp.maximum(x, 0.0) + jnp.log1p(jnp.exp(-jnp.abs(x)))


def _dot(a, b):
    return jnp.dot(a, b, preferred_element_type=F32)


def _dot_nt(a, b):
    return lax.dot_general(a, b, (((1,), (1,)), ((), ())), preferred_element_type=F32)


def _split2_rows(x):
    hi = x.astype(BF16)
    lo = (x - hi.astype(F32)).astype(BF16)
    return jnp.concatenate([hi, lo], axis=0)


def _ffn_kernel(x_ref, wg_ref, wu_ref, wd_ref, g_ref, b_ref, o_ref, xb_ref):
    j = pl.program_id(1)
    last = pl.num_programs(1) - 1
    tm = x_ref.shape[0]

    def hidden_chunk(xb, weights=None):
        wg, wu, wd = weights or (wg_ref[...].astype(BF16), wu_ref[...].astype(BF16), wd_ref[...].astype(BF16))
        gate = _dot(xb, wg)
        up = _dot(xb, wu)
        act = (_silu(gate) * up).astype(BF16)
        return _dot(act, wd)

    @pl.when(j == 0)
    def _():
        xb = x_ref[...].astype(BF16)
        xb_ref[...] = xb
        o_ref[...] = hidden_chunk(xb)

    @pl.when((j > 0) & (j < last))
    def _():
        o_ref[...] += hidden_chunk(xb_ref[...])

    @pl.when((j > 0) & (j == last))
    def _():
        weights = (wg_ref[...].astype(BF16), wu_ref[...].astype(BF16), wd_ref[...].astype(BF16))
        for r0 in range(0, tm, FFN_LN_ROWS):
            rows = slice(r0, min(r0 + FFN_LN_ROWS, tm))
            acc = o_ref[rows, :] + hidden_chunk(xb_ref[rows, :], weights)
            y = DEEPNORM_ALPHA * x_ref[rows, :] + 0.5 * acc
            o_ref[rows, :] = _layer_norm(y, g_ref[...], b_ref[...])


def ffn_ln(x, w_gu, w_down, layer, g, b, tm=FFN_TM, tf=FFN_TF):
    t, d = x.shape
    ff = w_down.shape[1]
    tm = min(tm, t)
    nf = ff // tf
    assert nf > 1 and ff % tf == 0 and t % tm == 0
    return pl.pallas_call(
        _ffn_kernel,
        grid=(t // tm, nf),
        in_specs=[
            pl.BlockSpec((tm, d), lambda i, j: (i, 0)),
            pl.BlockSpec((None, d, tf), lambda i, j: (layer, 0, j)),
            pl.BlockSpec((None, d, tf), lambda i, j: (layer, 0, j + nf)),
            pl.BlockSpec((None, tf, d), lambda i, j: (layer, j, 0)),
            pl.BlockSpec((1, d), lambda i, j: (0, 0)),
            pl.BlockSpec((1, d), lambda i, j: (0, 0)),
        ],
        out_specs=pl.BlockSpec((tm, d), lambda i, j: (i, 0)),
        out_shape=jax.ShapeDtypeStruct((t, d), F32),
        scratch_shapes=[pltpu.VMEM((tm, d), BF16)],
        compiler_params=_cparams(("parallel", "arbitrary")),
        name="ffn_ln",
    )(x, w_gu, w_gu, w_down, g, b)


def _proj_kernel(x_ref, wk_ref, wt_ref, wc_ref, wg_ref,
                 ka_ref, kb_ref, qta_ref, vta_ref, qtb_ref, vtb_ref, c_ref, gate_ref):
    xb = x_ref[...].astype(BF16)
    c_ref[...] = _dot(xb, wc_ref[...])
    k = _dot(xb, wk_ref[...])
    ka_ref[...] = k[:, :A_WIDTH].astype(BF16)
    kb_ref[...] = k[:, A_WIDTH:].astype(BF16)
    t = _dot_nt(wt_ref[...], xb)
    for s in range(qta_ref.shape[0]):
        cs = slice(s * ATT_T, (s + 1) * ATT_T)
        g = GROUP_W
        qta_ref[s] = (t[0:g, cs] * (A_QK_DIM ** -0.5 * LOG2_E)).astype(BF16)
        vta_ref[s] = t[g:2 * g, cs].astype(BF16)
        qtb_ref[s] = (t[2 * g:3 * g, cs] * (B_HEAD_DIM ** -0.5)).astype(BF16)
        vtb_ref[s] = t[3 * g:4 * g, cs].astype(BF16)
    gate_ref[...] = _dot(xb, wg_ref[...])


def in_proj(x, wk, wt, wc, wg, tm=PROJ_TM):
    t, d = x.shape
    tm = min(tm, t)
    nt = t // tm
    full = lambda a: pl.BlockSpec(a.shape, lambda i: (0,) * a.ndim)
    row = lambda w: pl.BlockSpec((tm, w), lambda i: (i, 0))
    tr = pl.BlockSpec((tm // ATT_T, GROUP_W, ATT_T), lambda i: (i, 0, 0))
    tr_shape = jax.ShapeDtypeStruct((t // ATT_T, GROUP_W, ATT_T), BF16)
    return pl.pallas_call(
        _proj_kernel,
        grid=(nt,),
        in_specs=[row(d), full(wk), full(wt), full(wc), full(wg)],
        out_specs=[row(GROUP_W), row(GROUP_W), tr, tr, tr, tr, row(wc.shape[1]), row(V7X_LANES)],
        out_shape=[
            jax.ShapeDtypeStruct((t, GROUP_W), BF16), jax.ShapeDtypeStruct((t, GROUP_W), BF16),
            tr_shape, tr_shape, tr_shape, tr_shape,
            jax.ShapeDtypeStruct((t, wc.shape[1]), F32), jax.ShapeDtypeStruct((t, V7X_LANES), F32),
        ],
        compiler_params=_cparams(("parallel",)),
        name="in_proj",
    )(x, wk, wt, wc, wg)


def _diff_attn_kernel(q_ref, k_ref, v_ref, lam_ref, g_ref, o_ref,
                      km_ref, m_ref, a_ref, acc_ref, s_ref, tm_ref, p_ref, ot_ref, *, lambda_init, tile, nq):
    qi = pl.program_id(1)
    nc = 2 * A_HEADS
    qt = q_ref[0]

    @pl.when(qi == 0)
    def _():
        lane = lax.broadcasted_iota(jnp.int32, (tile, GROUP_W), 1)
        for j in range(nq):
            kj = k_ref[j * tile:(j + 1) * tile, :]
            for c in range(nc):
                keep = (lane >= c * A_QK_DIM) & (lane < (c + 1) * A_QK_DIM)
                km_ref[j, c * tile:(c + 1) * tile, :] = jnp.where(keep, kj, jnp.zeros_like(kj))

    def scores(j, slot):
        s = _dot(km_ref[j], qt)
        s_ref[slot] = s
        for c in range(nc):
            tm_ref[slot, c] = jnp.max(s[c * tile:(c + 1) * tile], axis=0, keepdims=True)

    m_ref[...] = jnp.full(m_ref.shape, NEG_BIG, F32)
    acc_ref[...] = jnp.zeros_like(acc_ref)
    ones = jnp.ones((ONES_ROWS, tile), BF16)

    def step(cur, src, nxt, dst, masked):
        vt = v_ref[cur]
        if masked:
            key = lax.broadcasted_iota(jnp.int32, (tile, tile), 0)
            qry = lax.broadcasted_iota(jnp.int32, (tile, tile), 1)
            causal = key <= qry
        if nxt is not None:
            scores(nxt, dst)
        m_new = []
        for c in range(nc):
            if masked:
                tmax = jnp.max(jnp.where(causal, s_ref[src, c * tile:(c + 1) * tile, :], NEG_BIG),
                               axis=0, keepdims=True)
            else:
                tmax = tm_ref[src, c]
            m_old = m_ref[c]
            m_new.append(jnp.maximum(m_old, tmax))
            a_ref[c] = jnp.exp2(m_old - m_new[c])
            m_ref[c] = m_new[c]

        def pv(c):
            h = c // 2
            vt_ext = jnp.concatenate([vt[h * A_V_DIM:(h + 1) * A_V_DIM, :], ones], axis=0)
            acc_ref[c] = a_ref[c] * acc_ref[c] + _dot(vt_ext, p_ref[c])

        for c in range(nc):
            s = s_ref[src, c * tile:(c + 1) * tile, :]
            if masked:
                s = jnp.where(causal, s, NEG_BIG)
            p_ref[c] = jnp.exp2((s - m_new[c]).astype(BF16))
            if c > 0:
                pv(c - 1)
        pv(nc - 1)

    scores(0, 0)

    def body(jj, carry):
        j = 2 * jj
        step(j, 0, j + 1, 1, False)
        step(j + 1, 1, j + 2, 0, False)
        return carry

    lax.fori_loop(0, qi // 2, body, 0)

    @pl.when(qi % 2 == 0)
    def _():
        step(qi, 0, None, None, True)

    @pl.when(qi % 2 == 1)
    def _():
        step(qi - 1, 0, qi, 1, False)
        step(qi, 1, None, None, True)

    lf = lam_ref[...]
    lam = (jnp.exp(jnp.sum(lf[0:1] * lf[1:2], axis=-1, keepdims=True))
           - jnp.exp(jnp.sum(lf[2:3] * lf[3:4], axis=-1, keepdims=True)) + lambda_init)
    g = g_ref[...]
    dv = A_V_DIM
    for h in range(A_HEADS):
        a1 = acc_ref[2 * h]
        a2 = acc_ref[2 * h + 1]
        o = a1[:dv] / a1[dv:dv + 1] - lam * (a2[:dv] / a2[dv:dv + 1])
        ms = jnp.mean(o * o, axis=0, keepdims=True)
        ot_ref[h * A_V_DIM:(h + 1) * A_V_DIM, :] = o * lax.rsqrt(ms + RMS_EPS) * g * (1.0 - lambda_init)
    o_ref[...] = ot_ref[...].T.astype(BF16)


def diff_attn(qt, k, vt, diff_lambda, norm_g, batch, lambda_init, tile=ATT_T):
    t = k.shape[0]
    s = t // batch
    nq = s // tile
    nc = 2 * A_HEADS
    kern = functools.partial(_diff_attn_kernel, lambda_init=lambda_init, tile=tile, nq=nq)
    return pl.pallas_call(
        kern,
        grid=(batch, nq),
        in_specs=[
            pl.BlockSpec((1, GROUP_W, tile), lambda b, i: (b * nq + i, 0, 0)),
            pl.BlockSpec((s, GROUP_W), lambda b, i: (b, 0)),
            pl.BlockSpec((nq, GROUP_W, tile), lambda b, i: (b, 0, 0)),
            pl.BlockSpec(diff_lambda.shape, lambda b, i: (0, 0)),
            pl.BlockSpec((A_V_DIM, 1), lambda b, i: (0, 0)),
        ],
        out_specs=pl.BlockSpec((tile, GROUP_W), lambda b, i: (b * nq + i, 0)),
        out_shape=jax.ShapeDtypeStruct((t, GROUP_W), BF16),
        scratch_shapes=[
            pltpu.VMEM((nq, nc * tile, GROUP_W), BF16),
            pltpu.VMEM((nc, 1, tile), F32),
            pltpu.VMEM((nc, 1, tile), F32),
            pltpu.VMEM((nc, A_V_DIM + ONES_ROWS, tile), F32),
            pltpu.VMEM((2, nc * tile, tile), F32),
            pltpu.VMEM((2, nc, 1, tile), F32),
            pltpu.VMEM((nc, tile, tile), BF16),
            pltpu.VMEM((GROUP_W, tile), F32),
        ],
        compiler_params=_cparams(("parallel", "arbitrary")),
        name="diff_attn",
    )(qt, k, vt, diff_lambda, norm_g.reshape(A_V_DIM, 1))


def _stick_attn_kernel(q_ref, k_ref, v_ref, g_ref, o_ref,
                       qm_ref, carry_ref, acc_ref, z_ref, hl_ref, t_ref, w_ref, ot_ref, *, tile):
    qi = pl.program_id(1)
    qt = q_ref[0]
    row = lax.broadcasted_iota(jnp.int32, qt.shape, 0)
    for h in range(B_HEADS):
        keep = (row >= h * B_HEAD_DIM) & (row < (h + 1) * B_HEAD_DIM)
        qm_ref[h] = jnp.where(keep, qt, jnp.zeros_like(qt))
    carry_ref[...] = jnp.zeros_like(carry_ref)
    acc_ref[...] = jnp.zeros_like(acc_ref)

    key = lax.broadcasted_iota(jnp.int32, (tile, tile), 0)
    qry = lax.broadcasted_iota(jnp.int32, (tile, tile), 1)
    upper = (qry > key).astype(BF16)
    upper2 = jnp.concatenate([upper, upper], axis=1)
    strict = key < qry

    def do_tiles(tiles):
        slots = range(len(tiles))
        for ti in slots:
            j = tiles[ti][0]
            kb = k_ref[pl.ds(pl.multiple_of(j * tile, tile), tile), :]
            for h in range(B_HEADS):
                z_ref[ti, h] = _dot(kb, qm_ref[h])
        csum = {}
        for ti in slots:
            for h in range(B_HEADS):
                z = z_ref[ti, h]
                sp = jnp.maximum(z, 0.0) + jnp.log(1.0 + jnp.exp(-jnp.abs(z)))
                if tiles[ti][1]:
                    sp = jnp.where(strict, sp, 0.0)
                hi = sp.astype(BF16)
                hl_ref[ti, h, 0:tile, :] = hi
                hl_ref[ti, h, tile:, :] = (sp - hi.astype(F32)).astype(BF16)
                z_ref[ti, h] = z - sp
                csum[ti, h] = jnp.sum(sp, axis=0, keepdims=True)
        before = {}
        for h in range(B_HEADS):
            carry = carry_ref[h]
            for ti in slots:
                before[ti, h] = carry
                carry = carry + csum[ti, h]
            carry_ref[h] = carry
        for ti in slots:
            for h in range(B_HEADS):
                t_ref[ti, h] = _dot(upper2, hl_ref[ti, h])
        for ti in slots:
            for h in range(B_HEADS):
                w = jnp.exp(z_ref[ti, h] - t_ref[ti, h] - before[ti, h])
                if tiles[ti][1]:
                    w = jnp.where(strict, w, 0.0)
                w_ref[ti, h] = w.astype(BF16)
        for ti in slots:
            vt = v_ref[tiles[ti][0]]
            for h in range(B_HEADS):
                acc_ref[h] += _dot(vt[h * B_HEAD_DIM:(h + 1) * B_HEAD_DIM, :], w_ref[ti, h])

    @pl.when(qi == 0)
    def _():
        do_tiles([(qi, True)])

    @pl.when(qi > 0)
    def _():
        do_tiles([(qi, True), (qi - 1, False)])

    def live():
        return jnp.min(carry_ref[...]) < STICK_DEAD_LOG

    def cond(state):
        jj, go = state
        return jnp.logical_and(jj < qi, go)

    def body(state):
        jj, _ = state
        do_tiles([(qi - 1 - jj, False)])
        return jj + 1, live()

    lax.while_loop(cond, body, (jnp.int32(1), live()))

    g = g_ref[...]
    for h in range(B_HEADS):
        o = acc_ref[h]
        ms = jnp.mean(o * o, axis=0, keepdims=True)
        ot_ref[h * B_HEAD_DIM:(h + 1) * B_HEAD_DIM, :] = o * lax.rsqrt(ms + RMS_EPS) * g
    o_ref[...] = ot_ref[...].T.astype(BF16)


def stick_attn(qt, k, vt, norm_g, batch, tile=ATT_T):
    t = k.shape[0]
    s = t // batch
    nq = s // tile
    kern = functools.partial(_stick_attn_kernel, tile=tile)
    return pl.pallas_call(
        kern,
        grid=(batch, nq),
        in_specs=[
            pl.BlockSpec((1, GROUP_W, tile), lambda b, i: (b * nq + i, 0, 0)),
            pl.BlockSpec((s, GROUP_W), lambda b, i: (b, 0)),
            pl.BlockSpec((nq, GROUP_W, tile), lambda b, i: (b, 0, 0)),
            pl.BlockSpec((B_HEAD_DIM, 1), lambda b, i: (0, 0)),
        ],
        out_specs=pl.BlockSpec((tile, GROUP_W), lambda b, i: (b * nq + i, 0)),
        out_shape=jax.ShapeDtypeStruct((t, GROUP_W), BF16),
        scratch_shapes=[
            pltpu.VMEM((B_HEADS, GROUP_W, tile), BF16),
            pltpu.VMEM((B_HEADS, 1, tile), F32),
            pltpu.VMEM((B_HEADS, B_HEAD_DIM, tile), F32),
            pltpu.VMEM((2, B_HEADS, tile, tile), F32),
            pltpu.VMEM((2, B_HEADS, 2 * tile, tile), BF16),
            pltpu.VMEM((2, B_HEADS, tile, tile), F32),
            pltpu.VMEM((2, B_HEADS, tile, tile), BF16),
            pltpu.VMEM((GROUP_W, tile), F32),
        ],
        compiler_params=_cparams(("parallel", "arbitrary")),
        name="stick_attn",
    )(qt, k, vt, norm_g.reshape(B_HEAD_DIM, 1))


MF_STRICT, MF_NSTRICT, MF_INCL, MF_EYE, MF_PAIR = range(5)


def _gdn_masks(c):
    r = np.arange(c)[:, None]
    q = np.arange(c)[None, :]
    strict = (q < r).astype(np.float32)
    incl = (q <= r).astype(np.float32)
    eye = (q == r).astype(np.float32)
    pair = ((r >> 1) == (q >> 1)).astype(np.float32)
    mf = np.stack([strict, -strict, incl, eye, pair])
    joins = []
    bs = 2
    while bs < c:
        sh = bs.bit_length() - 1
        joins.append((((r >> (sh + 1)) == (q >> (sh + 1))) & ((r >> sh) != (q >> sh))).astype(np.float32))
        bs *= 2
    joins.append(1.0 - eye)
    return (jnp.asarray(mf), jnp.asarray(np.concatenate([incl, incl], axis=1), dtype=BF16),
            jnp.asarray(np.stack(joins), dtype=BF16))


def _gdn_kernel(x_ref, halo_ref, cw_ref, gate_ref, alog_ref, dtb_ref, z_ref, gn_ref, mf_ref, incl2_ref, mb_ref,
                o_ref, xs_ref, y_ref, s_ref, *, chunk, per_step):
    gi = pl.program_id(1)

    @pl.when(gi == 0)
    def _():
        s_ref[...] = jnp.zeros_like(s_ref)

    c = chunk
    dh = C_HEAD_DIM
    n_levels = mb_ref.shape[0] - 1

    xs_ref[0:V7X_SUBLANES, :] = jnp.where(gi > 0, halo_ref[...], 0.0)
    xs_ref[V7X_SUBLANES:, :] = x_ref[...]
    for c0 in range(0, 3 * C_WIDTH, CONV_COLS):
        cols = slice(c0, c0 + CONV_COLS)
        xs = xs_ref[:, cols]
        conv = cw_ref[CONV_K - 1:CONV_K, cols] * xs[V7X_SUBLANES:]
        for back in range(1, CONV_K):
            shifted = pltpu.roll(xs, back, axis=0)[V7X_SUBLANES:]
            conv = conv + cw_ref[CONV_K - 1 - back:CONV_K - back, cols] * shifted
        y_ref[:, cols] = _silu(conv)

    gates = gate_ref[...]
    beta_all = _sigmoid(gates)
    g_all = -jnp.exp(alog_ref[...]) * _softplus(gates + dtb_ref[...])
    scale = dh ** -0.5

    chains = [(ck, h) for ck in range(per_step) for h in range(C_HEADS)]
    qn, kn, kbeta, vbeta, gwide = {}, {}, {}, {}, {}
    for key in chains:
        ck, h = key
        rs = slice(ck * c, (ck + 1) * c)
        q = y_ref[rs, h * dh:(h + 1) * dh]
        k = y_ref[rs, C_WIDTH + h * dh:C_WIDTH + (h + 1) * dh]
        v = y_ref[rs, 2 * C_WIDTH + h * dh:2 * C_WIDTH + (h + 1) * dh]
        qn[key] = q * lax.rsqrt(jnp.sum(q * q, axis=-1, keepdims=True) + RMS_EPS)
        k = k * lax.rsqrt(jnp.sum(k * k, axis=-1, keepdims=True) + RMS_EPS)
        kn[key] = k
        beta_b = jnp.broadcast_to(beta_all[rs, h:h + 1], (c, dh))
        kbeta[key] = k * beta_b
        vbeta[key] = v * beta_b
        gwide[key] = jnp.broadcast_to(g_all[rs, C_HEADS + h:C_HEADS + h + 1], (c, c))

    gdiff = {key: _dot(incl2_ref[...], _split2_rows(gwide[key] * mf_ref[MF_STRICT])) for key in chains}
    gc = {key: jnp.broadcast_to(gdiff[key][:, 0:1] + gwide[key][0:1, 0:1], (c, dh)) for key in chains}
    kf = {key: kn[key].astype(BF16) for key in chains}
    akk = {key: _dot_nt(kbeta[key].astype(BF16), kf[key]) for key in chains}
    aqk = {key: _dot_nt((qn[key] * scale).astype(BF16), kf[key]) for key in chains}
    nlm, xb, intra = {}, {}, {}
    for key in chains:
        e = jnp.exp(gdiff[key])
        nl = akk[key] * (e * mf_ref[MF_NSTRICT])
        nlm[key] = nl.astype(BF16)
        intra[key] = (aqk[key] * (e * mf_ref[MF_INCL])).astype(BF16)
        xb[key] = (nl * mf_ref[MF_PAIR] + mf_ref[MF_EYE]).astype(BF16)

    for lvl in range(n_levels):
        zb = {key: (_dot(xb[key], nlm[key] * mb_ref[lvl]) + mf_ref[MF_EYE]).astype(BF16) for key in chains}
        xb = {key: _dot(zb[key], xb[key]).astype(BF16) for key in chains}

    egc = {key: jnp.exp(gc[key]) for key in chains}
    rhs = {key: jnp.concatenate([vbeta[key], kbeta[key] * egc[key]], axis=1) for key in chains}
    sol = {key: rhs[key] + _dot(xb[key] * mb_ref[n_levels], rhs[key].astype(BF16)) for key in chains}
    wb = {key: sol[key][:, dh:].astype(BF16) for key in chains}
    qg = {key: (qn[key] * egc[key] * scale).astype(BF16) for key in chains}
    kdt, etot = {}, {}
    for key in chains:
        total = gc[key][c - 1:c, :]
        kdt[key] = (kn[key] * jnp.exp(total - gc[key])).T.astype(BF16)
        etot[key] = jnp.exp(total)

    gn = gn_ref[...]
    for ck in range(per_step):
        keys = [(ck, h) for h in range(C_HEADS)]
        rs = slice(ck * c, (ck + 1) * c)
        state = {key: s_ref[key[1]] for key in keys}
        sb = {key: state[key].astype(BF16) for key in keys}
        ws = {key: _dot(wb[key], sb[key]) for key in keys}
        qs = {key: _dot(qg[key], sb[key]) for key in keys}
        vnb = {key: (sol[key][:, :dh] - ws[key]).astype(BF16) for key in keys}
        o = {key: qs[key] + _dot(intra[key], vnb[key]) for key in keys}
        for key in keys:
            s_ref[key[1]] = state[key] * etot[key] + _dot(kdt[key], vnb[key])
        for key in keys:
            sl = slice(key[1] * dh, (key[1] + 1) * dh)
            on = o[key] * lax.rsqrt(jnp.mean(o[key] * o[key], axis=-1, keepdims=True) + RMS_EPS) * gn
            o_ref[rs, sl] = (on * _silu(z_ref[rs, sl])).astype(BF16)


def gdn(cbuf, gates, conv_w, alog_row, dtb_row, norm_g, batch, chunk=GDN_C, per_step=GDN_CHUNKS_PER_STEP):
    t = cbuf.shape[0]
    rows = chunk * per_step
    ng = t // batch // rows
    hb = rows // V7X_SUBLANES
    w3 = 3 * C_WIDTH
    mf, incl2, mb = _gdn_masks(chunk)
    kern = functools.partial(_gdn_kernel, chunk=chunk, per_step=per_step)
    row = lambda w: pl.BlockSpec((rows, w), lambda b, i: (b * ng + i, 0))
    const = lambda a: pl.BlockSpec(a.shape, lambda b, i: (0,) * a.ndim)
    return pl.pallas_call(
        kern,
        grid=(batch, ng),
        in_specs=[
            row(w3),
            pl.BlockSpec((V7X_SUBLANES, w3), lambda b, i: (jnp.maximum((b * ng + i) * hb - 1, 0), 0)),
            const(conv_w),
            row(V7X_LANES),
            pl.BlockSpec((1, V7X_LANES), lambda b, i: (0, 0)),
            pl.BlockSpec((1, V7X_LANES), lambda b, i: (0, 0)),
            pl.BlockSpec((rows, C_WIDTH), lambda b, i: (b * ng + i, 3)),
            pl.BlockSpec((1, C_HEAD_DIM), lambda b, i: (0, 0)),
            const(mf), const(incl2), const(mb),
        ],
        out_specs=row(C_WIDTH),
        out_shape=jax.ShapeDtypeStruct((t, C_WIDTH), BF16),
        scratch_shapes=[pltpu.VMEM((rows + V7X_SUBLANES, w3), F32),
                        pltpu.VMEM((rows, w3), F32),
                        pltpu.VMEM((C_HEADS, C_HEAD_DIM, C_HEAD_DIM), F32)],
        compiler_params=_cparams(("parallel", "arbitrary")),
        name="gdn",
    )(cbuf, cbuf, conv_w, gates, alog_row, dtb_row, cbuf, norm_g.reshape(1, C_HEAD_DIM), mf, incl2, mb)


def _out_kernel(x_ref, oa_ref, ob_ref, oc_ref, w_ref, g_ref, b_ref, o_ref):
    tm = x_ref.shape[0]
    for r0 in range(0, tm, OUT_LN_ROWS):
        rows = slice(r0, min(r0 + OUT_LN_ROWS, tm))
        mix = _dot(oa_ref[rows, :], w_ref[0:A_WIDTH, :])
        mix = mix + _dot(ob_ref[rows, :], w_ref[A_WIDTH:A_WIDTH + B_WIDTH, :])
        mix = mix + _dot(oc_ref[rows, :], w_ref[A_WIDTH + B_WIDTH:, :])
        y = DEEPNORM_ALPHA * x_ref[rows, :] + mix
        o_ref[rows, :] = _layer_norm(y, g_ref[...], b_ref[...])


def out_ln(x, oa, ob, oc, w_out, g, b, tm=OUT_TM):
    t, d = x.shape
    tm = min(tm, t)
    row = lambda w: pl.BlockSpec((tm, w), lambda i: (i, 0))
    return pl.pallas_call(
        _out_kernel,
        grid=(t // tm,),
        in_specs=[row(d), row(A_WIDTH), row(B_WIDTH), row(C_WIDTH),
                  pl.BlockSpec(w_out.shape, lambda i: (0, 0)),
                  pl.BlockSpec((1, d), lambda i: (0, 0)), pl.BlockSpec((1, d), lambda i: (0, 0))],
        out_specs=row(d),
        out_shape=jax.ShapeDtypeStruct((t, d), F32),
        compiler_params=_cparams(("parallel",)),
        name="out_ln",
    )(x, oa, ob, oc, w_out, g, b)


def _split_w_in(w):
    o = 0
    parts = {}
    for name, n in (("qa", A_WIDTH), ("ka", A_WIDTH), ("va", A_WIDTH), ("qb", B_WIDTH), ("kb", B_WIDTH), ("vb", B_WIDTH),
                    ("qkv", 3 * C_WIDTH), ("z", C_WIDTH), ("beta", C_HEADS), ("a", C_HEADS)):
        parts[name] = w[:, o:o + n]
        o += n
    wk = jnp.concatenate([parts["ka"], parts["kb"]], axis=1).astype(BF16)
    wt = jnp.concatenate([parts["qa"], parts["va"], parts["qb"], parts["vb"]], axis=1).T.astype(BF16)
    wc = jnp.concatenate([parts["qkv"], parts["z"]], axis=1).astype(BF16)
    pad = jnp.zeros((w.shape[0], V7X_LANES - 2 * C_HEADS), w.dtype)
    wg = jnp.concatenate([parts["beta"], parts["a"], pad], axis=1).astype(BF16)
    return wk, wt, wc, wg


def _gate_row(vals):
    row = jnp.zeros((1, V7X_LANES), F32)
    return row.at[0, C_HEADS:2 * C_HEADS].set(vals.astype(F32))


def kernel(x, ffn1_w_gu, ffn1_w_down, ffn2_w_gu, ffn2_w_down, ln_g, ln_b, w_in, conv_w, dn_a_log, dn_dt_bias,
           dn_norm_g, diff_lambda, diff_norm_g, sb_norm_g, w_out):
    batch, seq, d = x.shape
    h = x.reshape(batch * seq, d)
    for l in range(DEPTH):
        lambda_init = 0.8 - 0.6 * math.exp(-0.3 * l)
        ln = lambda i: (ln_g[l, i].reshape(1, d), ln_b[l, i].reshape(1, d))
        h = ffn_ln(h, ffn1_w_gu, ffn1_w_down, l, *ln(0))
        wk, wt, wc, wg = _split_w_in(w_in[l])
        ka, kb, qta, vta, qtb, vtb, cbuf, gates = in_proj(h, wk, wt, wc, wg)
        oa = diff_attn(qta, ka, vta, diff_lambda[l], diff_norm_g[l], batch, lambda_init)
        ob = stick_attn(qtb, kb, vtb, sb_norm_g[l], batch)
        oc = gdn(cbuf, gates, conv_w[l], _gate_row(dn_a_log[l]), _gate_row(dn_dt_bias[l]), dn_norm_g[l], batch)
        h = out_ln(h, oa, ob, oc, w_out[l].astype(BF16), *ln(1))
        h = ffn_ln(h, ffn2_w_gu, ffn2_w_down, l, *ln(2))
    return h.reshape(batch, seq, d)
```

```python
import functools
import math

import jax
import jax.numpy as jnp
import numpy as np
from jax import lax
from jax.experimental import pallas as pl
from jax.experimental.pallas import tpu as pltpu

F32 = jnp.float32
BF16 = jnp.bfloat16

DEPTH = 2
A_HEADS, A_QK_DIM, A_V_DIM = 4, 32, 64
B_HEADS, B_HEAD_DIM = 4, 64
C_HEADS, C_HEAD_DIM = 4, 128
CONV_K = 4
A_WIDTH, B_WIDTH, C_WIDTH = 256, 256, 512
GROUP_W = 256
DEEPNORM_ALPHA = (2.0 * DEPTH) ** 0.25
LN_EPS = 1e-5
RMS_EPS = 1e-6
NEG_BIG = -1e30
LOG2_E = math.log2(math.e)
STICK_DEAD_LOG = 104.0

V7X_LANES = 128
V7X_SUBLANES = 8
V7X_VMEM_LIMIT_BYTES = 56 * 1024 * 1024
ONES_ROWS = 16

FFN_TM = 2048
FFN_TF = 256
FFN_LN_ROWS = 256
PROJ_TM = 1024
ATT_T = 256
GDN_C = 256
OUT_TM = 1024
OUT_LN_ROWS = 256
CONV_COLS = 512
GDN_CHUNKS_PER_STEP = 4


def _cparams(sem):
    return pltpu.CompilerParams(dimension_semantics=sem, vmem_limit_bytes=V7X_VMEM_LIMIT_BYTES)


def _layer_norm(y, g, b):
    mu = jnp.mean(y, axis=-1, keepdims=True)
    d = y - mu
    var = jnp.mean(d * d, axis=-1, keepdims=True)
    return d * lax.rsqrt(var + LN_EPS) * g + b


def _sigmoid(x):
    return 0.5 * jnp.tanh(0.5 * x) + 0.5


def _silu(x):
    return x * _sigmoid(x)


def _softplus(x):
    return jnp.maximum(x, 0.0) + jnp.log1p(jnp.exp(-jnp.abs(x)))


def _dot(a, b):
    return jnp.dot(a, b, preferred_element_type=F32)


def _dot_nt(a, b):
    return lax.dot_general(a, b, (((1,), (1,)), ((), ())), preferred_element_type=F32)


def _split2_rows(x):
    hi = x.astype(BF16)
    lo = (x - hi.astype(F32)).astype(BF16)
    return jnp.concatenate([hi, lo], axis=0)


def _ffn_kernel(x_ref, wg_ref, wu_ref, wd_ref, g_ref, b_ref, o_ref, xb_ref):
    j = pl.program_id(1)
    last = pl.num_programs(1) - 1
    tm = x_ref.shape[0]

    def hidden_chunk(xb, weights=None):
        wg, wu, wd = weights or (wg_ref[...].astype(BF16), wu_ref[...].astype(BF16), wd_ref[...].astype(BF16))
        gate = _dot(xb, wg)
        up = _dot(xb, wu)
        act = (_silu(gate) * up).astype(BF16)
        return _dot(act, wd)

    @pl.when(j == 0)
    def _():
        xb = x_ref[...].astype(BF16)
        xb_ref[...] = xb
        o_ref[...] = hidden_chunk(xb)

    @pl.when((j > 0) & (j < last))
    def _():
        o_ref[...] += hidden_chunk(xb_ref[...])

    @pl.when((j > 0) & (j == last))
    def _():
        weights = (wg_ref[...].astype(BF16), wu_ref[...].astype(BF16), wd_ref[...].astype(BF16))
        for r0 in range(0, tm, FFN_LN_ROWS):
            rows = slice(r0, min(r0 + FFN_LN_ROWS, tm))
            acc = o_ref[rows, :] + hidden_chunk(xb_ref[rows, :], weights)
            y = DEEPNORM_ALPHA * x_ref[rows, :] + 0.5 * acc
            o_ref[rows, :] = _layer_norm(y, g_ref[...], b_ref[...])


def ffn_ln(x, w_gu, w_down, layer, g, b, tm=FFN_TM, tf=FFN_TF):
    t, d = x.shape
    ff = w_down.shape[1]
    tm = min(tm, t)
    nf = ff // tf
    assert nf > 1 and ff % tf == 0 and t % tm == 0
    return pl.pallas_call(
        _ffn_kernel,
        grid=(t // tm, nf),
        in_specs=[
            pl.BlockSpec((tm, d), lambda i, j: (i, 0)),
            pl.BlockSpec((None, d, tf), lambda i, j: (layer, 0, j)),
            pl.BlockSpec((None, d, tf), lambda i, j: (layer, 0, j + nf)),
            pl.BlockSpec((None, tf, d), lambda i, j: (layer, j, 0)),
            pl.BlockSpec((1, d), lambda i, j: (0, 0)),
            pl.BlockSpec((1, d), lambda i, j: (0, 0)),
        ],
        out_specs=pl.BlockSpec((tm, d), lambda i, j: (i, 0)),
        out_shape=jax.ShapeDtypeStruct((t, d), F32),
        scratch_shapes=[pltpu.VMEM((tm, d), BF16)],
        compiler_params=_cparams(("parallel", "arbitrary")),
        name="ffn_ln",
    )(x, w_gu, w_gu, w_down, g, b)


def _proj_kernel(x_ref, wk_ref, wt_ref, wc_ref, wg_ref,
                 ka_ref, kb_ref, qta_ref, vta_ref, qtb_ref, vtb_ref, c_ref, gate_ref):
    xb = x_ref[...].astype(BF16)
    c_ref[...] = _dot(xb, wc_ref[...])
    k = _dot(xb, wk_ref[...])
    ka_ref[...] = k[:, :A_WIDTH].astype(BF16)
    kb_ref[...] = k[:, A_WIDTH:].astype(BF16)
    t = _dot_nt(wt_ref[...], xb)
    for s in range(qta_ref.shape[0]):
        cs = slice(s * ATT_T, (s + 1) * ATT_T)
        g = GROUP_W
        qta_ref[s] = (t[0:g, cs] * (A_QK_DIM ** -0.5 * LOG2_E)).astype(BF16)
        vta_ref[s] = t[g:2 * g, cs].astype(BF16)
        qtb_ref[s] = (t[2 * g:3 * g, cs] * (B_HEAD_DIM ** -0.5)).astype(BF16)
        vtb_ref[s] = t[3 * g:4 * g, cs].astype(BF16)
    gate_ref[...] = _dot(xb, wg_ref[...])


def in_proj(x, wk, wt, wc, wg, tm=PROJ_TM):
    t, d = x.shape
    tm = min(tm, t)
    nt = t // tm
    full = lambda a: pl.BlockSpec(a.shape, lambda i: (0,) * a.ndim)
    row = lambda w: pl.BlockSpec((tm, w), lambda i: (i, 0))
    tr = pl.BlockSpec((tm // ATT_T, GROUP_W, ATT_T), lambda i: (i, 0, 0))
    tr_shape = jax.ShapeDtypeStruct((t // ATT_T, GROUP_W, ATT_T), BF16)
    return pl.pallas_call(
        _proj_kernel,
        grid=(nt,),
        in_specs=[row(d), full(wk), full(wt), full(wc), full(wg)],
        out_specs=[row(GROUP_W), row(GROUP_W), tr, tr, tr, tr, row(wc.shape[1]), row(V7X_LANES)],
        out_shape=[
            jax.ShapeDtypeStruct((t, GROUP_W), BF16), jax.ShapeDtypeStruct((t, GROUP_W), BF16),
            tr_shape, tr_shape, tr_shape, tr_shape,
            jax.ShapeDtypeStruct((t, wc.shape[1]), F32), jax.ShapeDtypeStruct((t, V7X_LANES), F32),
        ],
        compiler_params=_cparams(("parallel",)),
        name="in_proj",
    )(x, wk, wt, wc, wg)


def _diff_attn_kernel(q_ref, k_ref, v_ref, lam_ref, g_ref, o_ref,
                      km_ref, m_ref, a_ref, acc_ref, s_ref, tm_ref, p_ref, ot_ref, *, lambda_init, tile, nq):
    qi = pl.program_id(1)
    nc = 2 * A_HEADS
    qt = q_ref[0]

    @pl.when(qi == 0)
    def _():
        lane = lax.broadcasted_iota(jnp.int32, (tile, GROUP_W), 1)
        for j in range(nq):
            kj = k_ref[j * tile:(j + 1) * tile, :]
            for c in range(nc):
                keep = (lane >= c * A_QK_DIM) & (lane < (c + 1) * A_QK_DIM)
                km_ref[j, c * tile:(c + 1) * tile, :] = jnp.where(keep, kj, jnp.zeros_like(kj))

    def scores(j, slot):
        s = _dot(km_ref[j], qt)
        s_ref[slot] = s
        for c in range(nc):
            tm_ref[slot, c] = jnp.max(s[c * tile:(c + 1) * tile], axis=0, keepdims=True)

    m_ref[...] = jnp.full(m_ref.shape, NEG_BIG, F32)
    acc_ref[...] = jnp.zeros_like(acc_ref)
    ones = jnp.ones((ONES_ROWS, tile), BF16)

    def step(cur, src, nxt, dst, masked):
        vt = v_ref[cur]
        if masked:
            key = lax.broadcasted_iota(jnp.int32, (tile, tile), 0)
            qry = lax.broadcasted_iota(jnp.int32, (tile, tile), 1)
            causal = key <= qry
        if nxt is not None:
            scores(nxt, dst)
        m_new = []
        for c in range(nc):
            if masked:
                tmax = jnp.max(jnp.where(causal, s_ref[src, c * tile:(c + 1) * tile, :], NEG_BIG),
                               axis=0, keepdims=True)
            else:
                tmax = tm_ref[src, c]
            m_old = m_ref[c]
            m_new.append(jnp.maximum(m_old, tmax))
            a_ref[c] = jnp.exp2(m_old - m_new[c])
            m_ref[c] = m_new[c]

        def pv(c):
            h = c // 2
            vt_ext = jnp.concatenate([vt[h * A_V_DIM:(h + 1) * A_V_DIM, :], ones], axis=0)
            acc_ref[c] = a_ref[c] * acc_ref[c] + _dot(vt_ext, p_ref[c])

        for c in range(nc):
            s = s_ref[src, c * tile:(c + 1) * tile, :]
            if masked:
                s = jnp.where(causal, s, NEG_BIG)
            p_ref[c] = jnp.exp2((s - m_new[c]).astype(BF16))
            if c > 0:
                pv(c - 1)
        pv(nc - 1)

    scores(0, 0)

    def body(jj, carry):
        j = 2 * jj
        step(j, 0, j + 1, 1, False)
        step(j + 1, 1, j + 2, 0, False)
        return carry

    lax.fori_loop(0, qi // 2, body, 0)

    @pl.when(qi % 2 == 0)
    def _():
        step(qi, 0, None, None, True)

    @pl.when(qi % 2 == 1)
    def _():
        step(qi - 1, 0, qi, 1, False)
        step(qi, 1, None, None, True)

    lf = lam_ref[...]
    lam = (jnp.exp(jnp.sum(lf[0:1] * lf[1:2], axis=-1, keepdims=True))
           - jnp.exp(jnp.sum(lf[2:3] * lf[3:4], axis=-1, keepdims=True)) + lambda_init)
    g = g_ref[...]
    dv = A_V_DIM
    for h in range(A_HEADS):
        a1 = acc_ref[2 * h]
        a2 = acc_ref[2 * h + 1]
        o = a1[:dv] / a1[dv:dv + 1] - lam * (a2[:dv] / a2[dv:dv + 1])
        ms = jnp.mean(o * o, axis=0, keepdims=True)
        ot_ref[h * A_V_DIM:(h + 1) * A_V_DIM, :] = o * lax.rsqrt(ms + RMS_EPS) * g * (1.0 - lambda_init)
    o_ref[...] = ot_ref[...].T.astype(BF16)


def diff_attn(qt, k, vt, diff_lambda, norm_g, batch, lambda_init, tile=ATT_T):
    t = k.shape[0]
    s = t // batch
    nq = s // tile
    nc = 2 * A_HEADS
    kern = functools.partial(_diff_attn_kernel, lambda_init=lambda_init, tile=tile, nq=nq)
    return pl.pallas_call(
        kern,
        grid=(batch, nq),
        in_specs=[
            pl.BlockSpec((1, GROUP_W, tile), lambda b, i: (b * nq + i, 0, 0)),
            pl.BlockSpec((s, GROUP_W), lambda b, i: (b, 0)),
            pl.BlockSpec((nq, GROUP_W, tile), lambda b, i: (b, 0, 0)),
            pl.BlockSpec(diff_lambda.shape, lambda b, i: (0, 0)),
            pl.BlockSpec((A_V_DIM, 1), lambda b, i: (0, 0)),
        ],
        out_specs=pl.BlockSpec((tile, GROUP_W), lambda b, i: (b * nq + i, 0)),
        out_shape=jax.ShapeDtypeStruct((t, GROUP_W), BF16),
        scratch_shapes=[
            pltpu.VMEM((nq, nc * tile, GROUP_W), BF16),
            pltpu.VMEM((nc, 1, tile), F32),
            pltpu.VMEM((nc, 1, tile), F32),
            pltpu.VMEM((nc, A_V_DIM + ONES_ROWS, tile), F32),
            pltpu.VMEM((2, nc * tile, tile), F32),
            pltpu.VMEM((2, nc, 1, tile), F32),
            pltpu.VMEM((nc, tile, tile), BF16),
            pltpu.VMEM((GROUP_W, tile), F32),
        ],
        compiler_params=_cparams(("parallel", "arbitrary")),
        name="diff_attn",
    )(qt, k, vt, diff_lambda, norm_g.reshape(A_V_DIM, 1))


def _stick_attn_kernel(q_ref, k_ref, v_ref, g_ref, o_ref,
                       qm_ref, carry_ref, acc_ref, z_ref, hl_ref, t_ref, w_ref, ot_ref, *, tile):
    qi = pl.program_id(1)
    qt = q_ref[0]
    row = lax.broadcasted_iota(jnp.int32, qt.shape, 0)
    for h in range(B_HEADS):
        keep = (row >= h * B_HEAD_DIM) & (row < (h + 1) * B_HEAD_DIM)
        qm_ref[h] = jnp.where(keep, qt, jnp.zeros_like(qt))
    carry_ref[...] = jnp.zeros_like(carry_ref)
    acc_ref[...] = jnp.zeros_like(acc_ref)

    key = lax.broadcasted_iota(jnp.int32, (tile, tile), 0)
    qry = lax.broadcasted_iota(jnp.int32, (tile, tile), 1)
    upper = (qry > key).astype(BF16)
    upper2 = jnp.concatenate([upper, upper], axis=1)
    strict = key < qry

    def do_tiles(tiles):
        slots = range(len(tiles))
        for ti in slots:
            j = tiles[ti][0]
            kb = k_ref[pl.ds(pl.multiple_of(j * tile, tile), tile), :]
            for h in range(B_HEADS):
                z_ref[ti, h] = _dot(kb, qm_ref[h])
        csum = {}
        for ti in slots:
            for h in range(B_HEADS):
                z = z_ref[ti, h]
                sp = jnp.maximum(z, 0.0) + jnp.log(1.0 + jnp.exp(-jnp.abs(z)))
                if tiles[ti][1]:
                    sp = jnp.where(strict, sp, 0.0)
                hi = sp.astype(BF16)
                hl_ref[ti, h, 0:tile, :] = hi
                hl_ref[ti, h, tile:, :] = (sp - hi.astype(F32)).astype(BF16)
                z_ref[ti, h] = z - sp
                csum[ti, h] = jnp.sum(sp, axis=0, keepdims=True)
        before = {}
        for h in range(B_HEADS):
            carry = carry_ref[h]
            for ti in slots:
                before[ti, h] = carry
                carry = carry + csum[ti, h]
            carry_ref[h] = carry
        for ti in slots:
            for h in range(B_HEADS):
                t_ref[ti, h] = _dot(upper2, hl_ref[ti, h])
        for ti in slots:
            for h in range(B_HEADS):
                w = jnp.exp(z_ref[ti, h] - t_ref[ti, h] - before[ti, h])
                if tiles[ti][1]:
                    w = jnp.where(strict, w, 0.0)
                w_ref[ti, h] = w.astype(BF16)
        for ti in slots:
            vt = v_ref[tiles[ti][0]]
            for h in range(B_HEADS):
                acc_ref[h] += _dot(vt[h * B_HEAD_DIM:(h + 1) * B_HEAD_DIM, :], w_ref[ti, h])

    @pl.when(qi == 0)
    def _():
        do_tiles([(qi, True)])

    @pl.when(qi > 0)
    def _():
        do_tiles([(qi, True), (qi - 1, False)])

    def live():
        return jnp.min(carry_ref[...]) < STICK_DEAD_LOG

    def cond(state):
        jj, go = state
        return jnp.logical_and(jj < qi, go)

    def body(state):
        jj, _ = state
        do_tiles([(qi - 1 - jj, False)])
        return jj + 1, live()

    lax.while_loop(cond, body, (jnp.int32(1), live()))

    g = g_ref[...]
    for h in range(B_HEADS):
        o = acc_ref[h]
        ms = jnp.mean(o * o, axis=0, keepdims=True)
        ot_ref[h * B_HEAD_DIM:(h + 1) * B_HEAD_DIM, :] = o * lax.rsqrt(ms + RMS_EPS) * g
    o_ref[...] = ot_ref[...].T.astype(BF16)


def stick_attn(qt, k, vt, norm_g, batch, tile=ATT_T):
    t = k.shape[0]
    s = t // batch
    nq = s // tile
    kern = functools.partial(_stick_attn_kernel, tile=tile)
    return pl.pallas_call(
        kern,
        grid=(batch, nq),
        in_specs=[
            pl.BlockSpec((1, GROUP_W, tile), lambda b, i: (b * nq + i, 0, 0)),
            pl.BlockSpec((s, GROUP_W), lambda b, i: (b, 0)),
            pl.BlockSpec((nq, GROUP_W, tile), lambda b, i: (b, 0, 0)),
            pl.BlockSpec((B_HEAD_DIM, 1), lambda b, i: (0, 0)),
        ],
        out_specs=pl.BlockSpec((tile, GROUP_W), lambda b, i: (b * nq + i, 0)),
        out_shape=jax.ShapeDtypeStruct((t, GROUP_W), BF16),
        scratch_shapes=[
            pltpu.VMEM((B_HEADS, GROUP_W, tile), BF16),
            pltpu.VMEM((B_HEADS, 1, tile), F32),
            pltpu.VMEM((B_HEADS, B_HEAD_DIM, tile), F32),
            pltpu.VMEM((2, B_HEADS, tile, tile), F32),
            pltpu.VMEM((2, B_HEADS, 2 * tile, tile), BF16),
            pltpu.VMEM((2, B_HEADS, tile, tile), F32),
            pltpu.VMEM((2, B_HEADS, tile, tile), BF16),
            pltpu.VMEM((GROUP_W, tile), F32),
        ],
        compiler_params=_cparams(("parallel", "arbitrary")),
        name="stick_attn",
    )(qt, k, vt, norm_g.reshape(B_HEAD_DIM, 1))


MF_STRICT, MF_NSTRICT, MF_INCL, MF_EYE, MF_PAIR = range(5)


def _gdn_masks(c):
    r = np.arange(c)[:, None]
    q = np.arange(c)[None, :]
    strict = (q < r).astype(np.float32)
    incl = (q <= r).astype(np.float32)
    eye = (q == r).astype(np.float32)
    pair = ((r >> 1) == (q >> 1)).astype(np.float32)
    mf = np.stack([strict, -strict, incl, eye, pair])
    joins = []
    bs = 2
    while bs < c:
        sh = bs.bit_length() - 1
        joins.append((((r >> (sh + 1)) == (q >> (sh + 1))) & ((r >> sh) != (q >> sh))).astype(np.float32))
        bs *= 2
    joins.append(1.0 - eye)
    return (jnp.asarray(mf), jnp.asarray(np.concatenate([incl, incl], axis=1), dtype=BF16),
            jnp.asarray(np.stack(joins), dtype=BF16))


def _gdn_kernel(x_ref, halo_ref, cw_ref, gate_ref, alog_ref, dtb_ref, z_ref, gn_ref, mf_ref, incl2_ref, mb_ref,
                o_ref, xs_ref, y_ref, s_ref, *, chunk, per_step):
    gi = pl.program_id(1)

    @pl.when(gi == 0)
    def _():
        s_ref[...] = jnp.zeros_like(s_ref)

    c = chunk
    dh = C_HEAD_DIM
    n_levels = mb_ref.shape[0] - 1

    xs_ref[0:V7X_SUBLANES, :] = jnp.where(gi > 0, halo_ref[...], 0.0)
    xs_ref[V7X_SUBLANES:, :] = x_ref[...]
    for c0 in range(0, 3 * C_WIDTH, CONV_COLS):
        cols = slice(c0, c0 + CONV_COLS)
        xs = xs_ref[:, cols]
        conv = cw_ref[CONV_K - 1:CONV_K, cols] * xs[V7X_SUBLANES:]
        for back in range(1, CONV_K):
            shifted = pltpu.roll(xs, back, axis=0)[V7X_SUBLANES:]
            conv = conv + cw_ref[CONV_K - 1 - back:CONV_K - back, cols] * shifted
        y_ref[:, cols] = _silu(conv)

    gates = gate_ref[...]
    beta_all = _sigmoid(gates)
    g_all = -jnp.exp(alog_ref[...]) * _softplus(gates + dtb_ref[...])
    scale = dh ** -0.5

    chains = [(ck, h) for ck in range(per_step) for h in range(C_HEADS)]
    qn, kn, kbeta, vbeta, gwide = {}, {}, {}, {}, {}
    for key in chains:
        ck, h = key
        rs = slice(ck * c, (ck + 1) * c)
        q = y_ref[rs, h * dh:(h + 1) * dh]
        k = y_ref[rs, C_WIDTH + h * dh:C_WIDTH + (h + 1) * dh]
        v = y_ref[rs, 2 * C_WIDTH + h * dh:2 * C_WIDTH + (h + 1) * dh]
        qn[key] = q * lax.rsqrt(jnp.sum(q * q, axis=-1, keepdims=True) + RMS_EPS)
        k = k * lax.rsqrt(jnp.sum(k * k, axis=-1, keepdims=True) + RMS_EPS)
        kn[key] = k
        beta_b = jnp.broadcast_to(beta_all[rs, h:h + 1], (c, dh))
        kbeta[key] = k * beta_b
        vbeta[key] = v * beta_b
        gwide[key] = jnp.broadcast_to(g_all[rs, C_HEADS + h:C_HEADS + h + 1], (c, c))

    gdiff = {key: _dot(incl2_ref[...], _split2_rows(gwide[key] * mf_ref[MF_STRICT])) for key in chains}
    gc = {key: jnp.broadcast_to(gdiff[key][:, 0:1] + gwide[key][0:1, 0:1], (c, dh)) for key in chains}
    kf = {key: kn[key].astype(BF16) for key in chains}
    akk = {key: _dot_nt(kbeta[key].astype(BF16), kf[key]) for key in chains}
    aqk = {key: _dot_nt((qn[key] * scale).astype(BF16), kf[key]) for key in chains}
    nlm, xb, intra = {}, {}, {}
    for key in chains:
        e = jnp.exp(gdiff[key])
        nl = akk[key] * (e * mf_ref[MF_NSTRICT])
        nlm[key] = nl.astype(BF16)
        intra[key] = (aqk[key] * (e * mf_ref[MF_INCL])).astype(BF16)
        xb[key] = (nl * mf_ref[MF_PAIR] + mf_ref[MF_EYE]).astype(BF16)

    for lvl in range(n_levels):
        zb = {key: (_dot(xb[key], nlm[key] * mb_ref[lvl]) + mf_ref[MF_EYE]).astype(BF16) for key in chains}
        xb = {key: _dot(zb[key], xb[key]).astype(BF16) for key in chains}

    egc = {key: jnp.exp(gc[key]) for key in chains}
    rhs = {key: jnp.concatenate([vbeta[key], kbeta[key] * egc[key]], axis=1) for key in chains}
    sol = {key: rhs[key] + _dot(xb[key] * mb_ref[n_levels], rhs[key].astype(BF16)) for key in chains}
    wb = {key: sol[key][:, dh:].astype(BF16) for key in chains}
    qg = {key: (qn[key] * egc[key] * scale).astype(BF16) for key in chains}
    kdt, etot = {}, {}
    for key in chains:
        total = gc[key][c - 1:c, :]
        kdt[key] = (kn[key] * jnp.exp(total - gc[key])).T.astype(BF16)
        etot[key] = jnp.exp(total)

    gn = gn_ref[...]
    for ck in range(per_step):
        keys = [(ck, h) for h in range(C_HEADS)]
        rs = slice(ck * c, (ck + 1) * c)
        state = {key: s_ref[key[1]] for key in keys}
        sb = {key: state[key].astype(BF16) for key in keys}
        ws = {key: _dot(wb[key], sb[key]) for key in keys}
        qs = {key: _dot(qg[key], sb[key]) for key in keys}
        vnb = {key: (sol[key][:, :dh] - ws[key]).astype(BF16) for key in keys}
        o = {key: qs[key] + _dot(intra[key], vnb[key]) for key in keys}
        for key in keys:
            s_ref[key[1]] = state[key] * etot[key] + _dot(kdt[key], vnb[key])
        for key in keys:
            sl = slice(key[1] * dh, (key[1] + 1) * dh)
            on = o[key] * lax.rsqrt(jnp.mean(o[key] * o[key], axis=-1, keepdims=True) + RMS_EPS) * gn
            o_ref[rs, sl] = (on * _silu(z_ref[rs, sl])).astype(BF16)


def gdn(cbuf, gates, conv_w, alog_row, dtb_row, norm_g, batch, chunk=GDN_C, per_step=GDN_CHUNKS_PER_STEP):
    t = cbuf.shape[0]
    rows = chunk * per_step
    ng = t // batch // rows
    hb = rows // V7X_SUBLANES
    w3 = 3 * C_WIDTH
    mf, incl2, mb = _gdn_masks(chunk)
    kern = functools.partial(_gdn_kernel, chunk=chunk, per_step=per_step)
    row = lambda w: pl.BlockSpec((rows, w), lambda b, i: (b * ng + i, 0))
    const = lambda a: pl.BlockSpec(a.shape, lambda b, i: (0,) * a.ndim)
    return pl.pallas_call(
        kern,
        grid=(batch, ng),
        in_specs=[
            row(w3),
            pl.BlockSpec((V7X_SUBLANES, w3), lambda b, i: (jnp.maximum((b * ng + i) * hb - 1, 0), 0)),
            const(conv_w),
            row(V7X_LANES),
            pl.BlockSpec((1, V7X_LANES), lambda b, i: (0, 0)),
            pl.BlockSpec((1, V7X_LANES), lambda b, i: (0, 0)),
            pl.BlockSpec((rows, C_WIDTH), lambda b, i: (b * ng + i, 3)),
            pl.BlockSpec((1, C_HEAD_DIM), lambda b, i: (0, 0)),
            const(mf), const(incl2), const(mb),
        ],
        out_specs=row(C_WIDTH),
        out_shape=jax.ShapeDtypeStruct((t, C_WIDTH), BF16),
        scratch_shapes=[pltpu.VMEM((rows + V7X_SUBLANES, w3), F32),
                        pltpu.VMEM((rows, w3), F32),
                        pltpu.VMEM((C_HEADS, C_HEAD_DIM, C_HEAD_DIM), F32)],
        compiler_params=_cparams(("parallel", "arbitrary")),
        name="gdn",
    )(cbuf, cbuf, conv_w, gates, alog_row, dtb_row, cbuf, norm_g.reshape(1, C_HEAD_DIM), mf, incl2, mb)


def _out_kernel(x_ref, oa_ref, ob_ref, oc_ref, w_ref, g_ref, b_ref, o_ref):
    tm = x_ref.shape[0]
    for r0 in range(0, tm, OUT_LN_ROWS):
        rows = slice(r0, min(r0 + OUT_LN_ROWS, tm))
        mix = _dot(oa_ref[rows, :], w_ref[0:A_WIDTH, :])
        mix = mix + _dot(ob_ref[rows, :], w_ref[A_WIDTH:A_WIDTH + B_WIDTH, :])
        mix = mix + _dot(oc_ref[rows, :], w_ref[A_WIDTH + B_WIDTH:, :])
        y = DEEPNORM_ALPHA * x_ref[rows, :] + mix
        o_ref[rows, :] = _layer_norm(y, g_ref[...], b_ref[...])


def out_ln(x, oa, ob, oc, w_out, g, b, tm=OUT_TM):
    t, d = x.shape
    tm = min(tm, t)
    row = lambda w: pl.BlockSpec((tm, w), lambda i: (i, 0))
    return pl.pallas_call(
        _out_kernel,
        grid=(t // tm,),
        in_specs=[row(d), row(A_WIDTH), row(B_WIDTH), row(C_WIDTH),
                  pl.BlockSpec(w_out.shape, lambda i: (0, 0)),
                  pl.BlockSpec((1, d), lambda i: (0, 0)), pl.BlockSpec((1, d), lambda i: (0, 0))],
        out_specs=row(d),
        out_shape=jax.ShapeDtypeStruct((t, d), F32),
        compiler_params=_cparams(("parallel",)),
        name="out_ln",
    )(x, oa, ob, oc, w_out, g, b)


def _split_w_in(w):
    o = 0
    parts = {}
    for name, n in (("qa", A_WIDTH), ("ka", A_WIDTH), ("va", A_WIDTH), ("qb", B_WIDTH), ("kb", B_WIDTH), ("vb", B_WIDTH),
                    ("qkv", 3 * C_WIDTH), ("z", C_WIDTH), ("beta", C_HEADS), ("a", C_HEADS)):
        parts[name] = w[:, o:o + n]
        o += n
    wk = jnp.concatenate([parts["ka"], parts["kb"]], axis=1).astype(BF16)
    wt = jnp.concatenate([parts["qa"], parts["va"], parts["qb"], parts["vb"]], axis=1).T.astype(BF16)
    wc = jnp.concatenate([parts["qkv"], parts["z"]], axis=1).astype(BF16)
    pad = jnp.zeros((w.shape[0], V7X_LANES - 2 * C_HEADS), w.dtype)
    wg = jnp.concatenate([parts["beta"], parts["a"], pad], axis=1).astype(BF16)
    return wk, wt, wc, wg


def _gate_row(vals):
    row = jnp.zeros((1, V7X_LANES), F32)
    return row.at[0, C_HEADS:2 * C_HEADS].set(vals.astype(F32))


def kernel(x, ffn1_w_gu, ffn1_w_down, ffn2_w_gu, ffn2_w_down, ln_g, ln_b, w_in, conv_w, dn_a_log, dn_dt_bias,
           dn_norm_g, diff_lambda, diff_norm_g, sb_norm_g, w_out):
    batch, seq, d = x.shape
    h = x.reshape(batch * seq, d)
    for l in range(DEPTH):
        lambda_init = 0.8 - 0.6 * math.exp(-0.3 * l)
        ln = lambda i: (ln_g[l, i].reshape(1, d), ln_b[l, i].reshape(1, d))
        h = ffn_ln(h, ffn1_w_gu, ffn1_w_down, l, *ln(0))
        wk, wt, wc, wg = _split_w_in(w_in[l])
        ka, kb, qta, vta, qtb, vtb, cbuf, gates = in_proj(h, wk, wt, wc, wg)
        oa = diff_attn(qta, ka, vta, diff_lambda[l], diff_norm_g[l], batch, lambda_init)
        ob = stick_attn(qtb, kb, vtb, sb_norm_g[l], batch)
        oc = gdn(cbuf, gates, conv_w[l], _gate_row(dn_a_log[l]), _gate_row(dn_dt_bias[l]), dn_norm_g[l], batch)
        h = out_ln(h, oa, ob, oc, w_out[l].astype(BF16), *ln(1))
        h = ffn_ln(h, ffn2_w_gu, ffn2_w_down, l, *ln(2))
    return h.reshape(batch, seq, d)
```

```python
import functools
import math

import jax
import jax.numpy as jnp
import numpy as np
from jax import lax
from jax.experimental import pallas as pl
from jax.experimental.pallas import tpu as pltpu

F32 = jnp.float32
BF16 = jnp.bfloat16

DEPTH = 2
A_HEADS, A_QK_DIM, A_V_DIM = 4, 32, 64
B_HEADS, B_HEAD_DIM = 4, 64
C_HEADS, C_HEAD_DIM = 4, 128
CONV_K = 4
A_WIDTH, B_WIDTH, C_WIDTH = 256, 256, 512
GROUP_W = 256
DEEPNORM_ALPHA = (2.0 * DEPTH) ** 0.25
LN_EPS = 1e-5
RMS_EPS = 1e-6
NEG_BIG = -1e30
LOG2_E = math.log2(math.e)
STICK_DEAD_LOG = 104.0

V7X_LANES = 128
V7X_SUBLANES = 8
V7X_VMEM_LIMIT_BYTES = 56 * 1024 * 1024
ONES_ROWS = 16

FFN_TM = 2048
FFN_TF = 256
FFN_MID_ROWS = 1024
FFN_LN_ROWS = 256
PROJ_TM = 1024
ATT_T = 256
GDN_C = 256
OUT_TM = 1024
OUT_LN_ROWS = 256
CONV_COLS = 512
GDN_CHUNKS_PER_STEP = 4


def _cparams(sem):
    return pltpu.CompilerParams(dimension_semantics=sem, vmem_limit_bytes=V7X_VMEM_LIMIT_BYTES)


def _layer_norm(y, g, b):
    mu = jnp.mean(y, axis=-1, keepdims=True)
    d = y - mu
    var = jnp.mean(d * d, axis=-1, keepdims=True)
    return d * lax.rsqrt(var + LN_EPS) * g + b


def _sigmoid(x):
    return 0.5 * jnp.tanh(0.5 * x) + 0.5


def _silu(x):
    return x * _sigmoid(x)


def _softplus(x):
    return jnp.maximum(x, 0.0) + jnp.log1p(jnp.exp(-jnp.abs(x)))


def _dot(a, b):
    return jnp.dot(a, b, preferred_element_type=F32)


def _dot_nt(a, b):
    return lax.dot_general(a, b, (((1,), (1,)), ((), ())), preferred_element_type=F32)


def _split2_rows(x):
    hi = x.astype(BF16)
    lo = (x - hi.astype(F32)).astype(BF16)
    return jnp.concatenate([hi, lo], axis=0)


def _ffn_kernel(x_ref, wg_ref, wu_ref, wd_ref, g_ref, b_ref, o_ref, xb_ref):
    j = pl.program_id(1)
    last = pl.num_programs(1) - 1
    tm = x_ref.shape[0]

    def hidden_chunk(xb, weights=None):
        wg, wu, wd = weights or (wg_ref[...].astype(BF16), wu_ref[...].astype(BF16), wd_ref[...].astype(BF16))
        gate = _dot(xb, wg)
        up = _dot(xb, wu)
        act = (_silu(gate) * up).astype(BF16)
        return _dot(act, wd)

    @pl.when(j == 0)
    def _():
        xb = x_ref[...].astype(BF16)
        xb_ref[...] = xb
        o_ref[...] = hidden_chunk(xb)

    @pl.when((j > 0) & (j < last))
    def _():
        weights = (wg_ref[...].astype(BF16), wu_ref[...].astype(BF16), wd_ref[...].astype(BF16))
        for r0 in range(0, tm, FFN_MID_ROWS):
            rows = slice(r0, min(r0 + FFN_MID_ROWS, tm))
            o_ref[rows, :] += hidden_chunk(xb_ref[rows, :], weights)

    @pl.when((j > 0) & (j == last))
    def _():
        weights = (wg_ref[...].astype(BF16), wu_ref[...].astype(BF16), wd_ref[...].astype(BF16))
        for r0 in range(0, tm, FFN_LN_ROWS):
            rows = slice(r0, min(r0 + FFN_LN_ROWS, tm))
            acc = o_ref[rows, :] + hidden_chunk(xb_ref[rows, :], weights)
            y = DEEPNORM_ALPHA * x_ref[rows, :] + 0.5 * acc
            o_ref[rows, :] = _layer_norm(y, g_ref[...], b_ref[...])


def ffn_ln(x, w_gu, w_down, layer, g, b, tm=FFN_TM, tf=FFN_TF):
    t, d = x.shape
    ff = w_down.shape[1]
    tm = min(tm, t)
    nf = ff // tf
    assert nf > 1 and ff % tf == 0 and t % tm == 0
    return pl.pallas_call(
        _ffn_kernel,
        grid=(t // tm, nf),
        in_specs=[
            pl.BlockSpec((tm, d), lambda i, j: (i, 0)),
            pl.BlockSpec((None, d, tf), lambda i, j: (layer, 0, j)),
            pl.BlockSpec((None, d, tf), lambda i, j: (layer, 0, j + nf)),
            pl.BlockSpec((None, tf, d), lambda i, j: (layer, j, 0)),
            pl.BlockSpec((1, d), lambda i, j: (0, 0)),
            pl.BlockSpec((1, d), lambda i, j: (0, 0)),
        ],
        out_specs=pl.BlockSpec((tm, d), lambda i, j: (i, 0)),
        out_shape=jax.ShapeDtypeStruct((t, d), F32),
        scratch_shapes=[pltpu.VMEM((tm, d), BF16)],
        compiler_params=_cparams(("parallel", "arbitrary")),
        name="ffn_ln",
    )(x, w_gu, w_gu, w_down, g, b)


def _proj_kernel(x_ref, wk_ref, wt_ref, wc_ref, wg_ref,
                 ka_ref, kb_ref, qta_ref, vta_ref, qtb_ref, vtb_ref, c_ref, gate_ref):
    xb = x_ref[...].astype(BF16)
    c_ref[...] = _dot(xb, wc_ref[...])
    k = _dot(xb, wk_ref[...])
    ka_ref[...] = k[:, :A_WIDTH].astype(BF16)
    kb_ref[...] = k[:, A_WIDTH:].astype(BF16)
    t = _dot_nt(wt_ref[...], xb)
    for s in range(qta_ref.shape[0]):
        cs = slice(s * ATT_T, (s + 1) * ATT_T)
        g = GROUP_W
        qta_ref[s] = (t[0:g, cs] * (A_QK_DIM ** -0.5 * LOG2_E)).astype(BF16)
        vta_ref[s] = t[g:2 * g, cs].astype(BF16)
        qtb_ref[s] = (t[2 * g:3 * g, cs] * (B_HEAD_DIM ** -0.5)).astype(BF16)
        vtb_ref[s] = t[3 * g:4 * g, cs].astype(BF16)
    gate_ref[...] = _dot(xb, wg_ref[...])


def in_proj(x, wk, wt, wc, wg, tm=PROJ_TM):
    t, d = x.shape
    tm = min(tm, t)
    nt = t // tm
    full = lambda a: pl.BlockSpec(a.shape, lambda i: (0,) * a.ndim)
    row = lambda w: pl.BlockSpec((tm, w), lambda i: (i, 0))
    tr = pl.BlockSpec((tm // ATT_T, GROUP_W, ATT_T), lambda i: (i, 0, 0))
    tr_shape = jax.ShapeDtypeStruct((t // ATT_T, GROUP_W, ATT_T), BF16)
    return pl.pallas_call(
        _proj_kernel,
        grid=(nt,),
        in_specs=[row(d), full(wk), full(wt), full(wc), full(wg)],
        out_specs=[row(GROUP_W), row(GROUP_W), tr, tr, tr, tr, row(wc.shape[1]), row(V7X_LANES)],
        out_shape=[
            jax.ShapeDtypeStruct((t, GROUP_W), BF16), jax.ShapeDtypeStruct((t, GROUP_W), BF16),
            tr_shape, tr_shape, tr_shape, tr_shape,
            jax.ShapeDtypeStruct((t, wc.shape[1]), F32), jax.ShapeDtypeStruct((t, V7X_LANES), F32),
        ],
        compiler_params=_cparams(("parallel",)),
        name="in_proj",
    )(x, wk, wt, wc, wg)


def _diff_attn_kernel(q_ref, k_ref, v_ref, lam_ref, g_ref, o_ref,
                      km_ref, m_ref, a_ref, acc_ref, s_ref, tm_ref, p_ref, ot_ref, *, lambda_init, tile, nq):
    qi = pl.program_id(1)
    nc = 2 * A_HEADS
    qt = q_ref[0]

    @pl.when(qi == 0)
    def _():
        lane = lax.broadcasted_iota(jnp.int32, (tile, GROUP_W), 1)
        for j in range(nq):
            kj = k_ref[j * tile:(j + 1) * tile, :]
            for c in range(nc):
                keep = (lane >= c * A_QK_DIM) & (lane < (c + 1) * A_QK_DIM)
                km_ref[j, c * tile:(c + 1) * tile, :] = jnp.where(keep, kj, jnp.zeros_like(kj))

    def scores(j, slot):
        s = _dot(km_ref[j], qt)
        s_ref[slot] = s
        for c in range(nc):
            tm_ref[slot, c] = jnp.max(s[c * tile:(c + 1) * tile], axis=0, keepdims=True)

    m_ref[...] = jnp.full(m_ref.shape, NEG_BIG, F32)
    acc_ref[...] = jnp.zeros_like(acc_ref)
    ones = jnp.ones((ONES_ROWS, tile), BF16)

    def step(cur, src, nxt, dst, masked):
        vt = v_ref[cur]
        if masked:
            key = lax.broadcasted_iota(jnp.int32, (tile, tile), 0)
            qry = lax.broadcasted_iota(jnp.int32, (tile, tile), 1)
            causal = key <= qry
        if nxt is not None:
            scores(nxt, dst)
        m_new = []
        for c in range(nc):
            if masked:
                tmax = jnp.max(jnp.where(causal, s_ref[src, c * tile:(c + 1) * tile, :], NEG_BIG),
                               axis=0, keepdims=True)
            else:
                tmax = tm_ref[src, c]
            m_old = m_ref[c]
            m_new.append(jnp.maximum(m_old, tmax))
            a_ref[c] = jnp.exp2(m_old - m_new[c])
            m_ref[c] = m_new[c]

        def pv(c):
            h = c // 2
            vt_ext = jnp.concatenate([vt[h * A_V_DIM:(h + 1) * A_V_DIM, :], ones], axis=0)
            acc_ref[c] = a_ref[c] * acc_ref[c] + _dot(vt_ext, p_ref[c])

        for c in range(nc):
            s = s_ref[src, c * tile:(c + 1) * tile, :]
            if masked:
                s = jnp.where(causal, s, NEG_BIG)
            p_ref[c] = jnp.exp2((s - m_new[c]).astype(BF16))
            if c > 0:
                pv(c - 1)
        pv(nc - 1)

    scores(0, 0)

    def body(jj, carry):
        j = 2 * jj
        step(j, 0, j + 1, 1, False)
        step(j + 1, 1, j + 2, 0, False)
        return carry

    lax.fori_loop(0, qi // 2, body, 0)

    @pl.when(qi % 2 == 0)
    def _():
        step(qi, 0, None, None, True)

    @pl.when(qi % 2 == 1)
    def _():
        step(qi - 1, 0, qi, 1, False)
        step(qi, 1, None, None, True)

    lf = lam_ref[...]
    lam = (jnp.exp(jnp.sum(lf[0:1] * lf[1:2], axis=-1, keepdims=True))
           - jnp.exp(jnp.sum(lf[2:3] * lf[3:4], axis=-1, keepdims=True)) + lambda_init)
    g = g_ref[...]
    dv = A_V_DIM
    for h in range(A_HEADS):
        a1 = acc_ref[2 * h]
        a2 = acc_ref[2 * h + 1]
        o = a1[:dv] / a1[dv:dv + 1] - lam * (a2[:dv] / a2[dv:dv + 1])
        ms = jnp.mean(o * o, axis=0, keepdims=True)
        ot_ref[h * A_V_DIM:(h + 1) * A_V_DIM, :] = o * lax.rsqrt(ms + RMS_EPS) * g * (1.0 - lambda_init)
    o_ref[...] = ot_ref[...].T.astype(BF16)


def diff_attn(qt, k, vt, diff_lambda, norm_g, batch, lambda_init, tile=ATT_T):
    t = k.shape[0]
    s = t // batch
    nq = s // tile
    nc = 2 * A_HEADS
    kern = functools.partial(_diff_attn_kernel, lambda_init=lambda_init, tile=tile, nq=nq)
    return pl.pallas_call(
        kern,
        grid=(batch, nq),
        in_specs=[
            pl.BlockSpec((1, GROUP_W, tile), lambda b, i: (b * nq + i, 0, 0)),
            pl.BlockSpec((s, GROUP_W), lambda b, i: (b, 0)),
            pl.BlockSpec((nq, GROUP_W, tile), lambda b, i: (b, 0, 0)),
            pl.BlockSpec(diff_lambda.shape, lambda b, i: (0, 0)),
            pl.BlockSpec((A_V_DIM, 1), lambda b, i: (0, 0)),
        ],
        out_specs=pl.BlockSpec((tile, GROUP_W), lambda b, i: (b * nq + i, 0)),
        out_shape=jax.ShapeDtypeStruct((t, GROUP_W), BF16),
        scratch_shapes=[
            pltpu.VMEM((nq, nc * tile, GROUP_W), BF16),
            pltpu.VMEM((nc, 1, tile), F32),
            pltpu.VMEM((nc, 1, tile), F32),
            pltpu.VMEM((nc, A_V_DIM + ONES_ROWS, tile), F32),
            pltpu.VMEM((2, nc * tile, tile), F32),
            pltpu.VMEM((2, nc, 1, tile), F32),
            pltpu.VMEM((nc, tile, tile), BF16),
            pltpu.VMEM((GROUP_W, tile), F32),
        ],
        compiler_params=_cparams(("parallel", "arbitrary")),
        name="diff_attn",
    )(qt, k, vt, diff_lambda, norm_g.reshape(A_V_DIM, 1))


def _stick_attn_kernel(q_ref, k_ref, v_ref, g_ref, o_ref,
                       qm_ref, carry_ref, acc_ref, z_ref, hl_ref, t_ref, w_ref, ot_ref, *, tile):
    qi = pl.program_id(1)
    qt = q_ref[0]
    row = lax.broadcasted_iota(jnp.int32, qt.shape, 0)
    for h in range(B_HEADS):
        keep = (row >= h * B_HEAD_DIM) & (row < (h + 1) * B_HEAD_DIM)
        qm_ref[h] = jnp.where(keep, qt, jnp.zeros_like(qt))
    carry_ref[...] = jnp.zeros_like(carry_ref)
    acc_ref[...] = jnp.zeros_like(acc_ref)

    key = lax.broadcasted_iota(jnp.int32, (tile, tile), 0)
    qry = lax.broadcasted_iota(jnp.int32, (tile, tile), 1)
    upper = (qry > key).astype(BF16)
    upper2 = jnp.concatenate([upper, upper], axis=1)
    strict = key < qry

    def do_tiles(tiles):
        slots = range(len(tiles))
        for ti in slots:
            j = tiles[ti][0]
            kb = k_ref[pl.ds(pl.multiple_of(j * tile, tile), tile), :]
            for h in range(B_HEADS):
                z_ref[ti, h] = _dot(kb, qm_ref[h])
        csum = {}
        for ti in slots:
            for h in range(B_HEADS):
                z = z_ref[ti, h]
                sp = jnp.maximum(z, 0.0) + jnp.log(1.0 + jnp.exp(-jnp.abs(z)))
                if tiles[ti][1]:
                    sp = jnp.where(strict, sp, 0.0)
                hi = sp.astype(BF16)
                hl_ref[ti, h, 0:tile, :] = hi
                hl_ref[ti, h, tile:, :] = (sp - hi.astype(F32)).astype(BF16)
                z_ref[ti, h] = z - sp
                csum[ti, h] = jnp.sum(sp, axis=0, keepdims=True)
        before = {}
        for h in range(B_HEADS):
            carry = carry_ref[h]
            for ti in slots:
                before[ti, h] = carry
                carry = carry + csum[ti, h]
            carry_ref[h] = carry
        for ti in slots:
            for h in range(B_HEADS):
                t_ref[ti, h] = _dot(upper2, hl_ref[ti, h])
        for ti in slots:
            for h in range(B_HEADS):
                w = jnp.exp(z_ref[ti, h] - t_ref[ti, h] - before[ti, h])
                if tiles[ti][1]:
                    w = jnp.where(strict, w, 0.0)
                w_ref[ti, h] = w.astype(BF16)
        for ti in slots:
            vt = v_ref[tiles[ti][0]]
            for h in range(B_HEADS):
                acc_ref[h] += _dot(vt[h * B_HEAD_DIM:(h + 1) * B_HEAD_DIM, :], w_ref[ti, h])

    @pl.when(qi == 0)
    def _():
        do_tiles([(qi, True)])

    @pl.when(qi > 0)
    def _():
        do_tiles([(qi, True), (qi - 1, False)])

    def live():
        return jnp.min(carry_ref[...]) < STICK_DEAD_LOG

    def cond(state):
        jj, go = state
        return jnp.logical_and(jj < qi, go)

    def body(state):
        jj, _ = state
        do_tiles([(qi - 1 - jj, False)])
        return jj + 1, live()

    lax.while_loop(cond, body, (jnp.int32(1), live()))

    g = g_ref[...]
    for h in range(B_HEADS):
        o = acc_ref[h]
        ms = jnp.mean(o * o, axis=0, keepdims=True)
        ot_ref[h * B_HEAD_DIM:(h + 1) * B_HEAD_DIM, :] = o * lax.rsqrt(ms + RMS_EPS) * g
    o_ref[...] = ot_ref[...].T.astype(BF16)


def stick_attn(qt, k, vt, norm_g, batch, tile=ATT_T):
    t = k.shape[0]
    s = t // batch
    nq = s // tile
    kern = functools.partial(_stick_attn_kernel, tile=tile)
    return pl.pallas_call(
        kern,
        grid=(batch, nq),
        in_specs=[
            pl.BlockSpec((1, GROUP_W, tile), lambda b, i: (b * nq + i, 0, 0)),
            pl.BlockSpec((s, GROUP_W), lambda b, i: (b, 0)),
            pl.BlockSpec((nq, GROUP_W, tile), lambda b, i: (b, 0, 0)),
            pl.BlockSpec((B_HEAD_DIM, 1), lambda b, i: (0, 0)),
        ],
        out_specs=pl.BlockSpec((tile, GROUP_W), lambda b, i: (b * nq + i, 0)),
        out_shape=jax.ShapeDtypeStruct((t, GROUP_W), BF16),
        scratch_shapes=[
            pltpu.VMEM((B_HEADS, GROUP_W, tile), BF16),
            pltpu.VMEM((B_HEADS, 1, tile), F32),
            pltpu.VMEM((B_HEADS, B_HEAD_DIM, tile), F32),
            pltpu.VMEM((2, B_HEADS, tile, tile), F32),
            pltpu.VMEM((2, B_HEADS, 2 * tile, tile), BF16),
            pltpu.VMEM((2, B_HEADS, tile, tile), F32),
            pltpu.VMEM((2, B_HEADS, tile, tile), BF16),
            pltpu.VMEM((GROUP_W, tile), F32),
        ],
        compiler_params=_cparams(("parallel", "arbitrary")),
        name="stick_attn",
    )(qt, k, vt, norm_g.reshape(B_HEAD_DIM, 1))


MF_STRICT, MF_NSTRICT, MF_INCL, MF_EYE, MF_PAIR = range(5)


def _gdn_masks(c):
    r = np.arange(c)[:, None]
    q = np.arange(c)[None, :]
    strict = (q < r).astype(np.float32)
    incl = (q <= r).astype(np.float32)
    eye = (q == r).astype(np.float32)
    pair = ((r >> 1) == (q >> 1)).astype(np.float32)
    mf = np.stack([strict, -strict, incl, eye, pair])
    joins = []
    bs = 2
    while bs < c:
        sh = bs.bit_length() - 1
        joins.append((((r >> (sh + 1)) == (q >> (sh + 1))) & ((r >> sh) != (q >> sh))).astype(np.float32))
        bs *= 2
    joins.append(1.0 - eye)
    return (jnp.asarray(mf), jnp.asarray(np.concatenate([incl, incl], axis=1), dtype=BF16),
            jnp.asarray(np.stack(joins), dtype=BF16))


def _gdn_kernel(x_ref, halo_ref, cw_ref, gate_ref, alog_ref, dtb_ref, z_ref, gn_ref, mf_ref, incl2_ref, mb_ref,
                o_ref, xs_ref, y_ref, s_ref, *, chunk, per_step):
    gi = pl.program_id(1)

    @pl.when(gi == 0)
    def _():
        s_ref[...] = jnp.zeros_like(s_ref)

    c = chunk
    dh = C_HEAD_DIM
    n_levels = mb_ref.shape[0] - 1

    xs_ref[0:V7X_SUBLANES, :] = jnp.where(gi > 0, halo_ref[...], 0.0)
    xs_ref[V7X_SUBLANES:, :] = x_ref[...]
    for c0 in range(0, 3 * C_WIDTH, CONV_COLS):
        cols = slice(c0, c0 + CONV_COLS)
        xs = xs_ref[:, cols]
        conv = cw_ref[CONV_K - 1:CONV_K, cols] * xs[V7X_SUBLANES:]
        for back in range(1, CONV_K):
            shifted = pltpu.roll(xs, back, axis=0)[V7X_SUBLANES:]
            conv = conv + cw_ref[CONV_K - 1 - back:CONV_K - back, cols] * shifted
        y_ref[:, cols] = _silu(conv)

    gates = gate_ref[...]
    beta_all = _sigmoid(gates)
    g_all = -jnp.exp(alog_ref[...]) * _softplus(gates + dtb_ref[...])
    scale = dh ** -0.5

    chains = [(ck, h) for ck in range(per_step) for h in range(C_HEADS)]
    qn, kn, kbeta, vbeta, gwide = {}, {}, {}, {}, {}
    for key in chains:
        ck, h = key
        rs = slice(ck * c, (ck + 1) * c)
        q = y_ref[rs, h * dh:(h + 1) * dh]
        k = y_ref[rs, C_WIDTH + h * dh:C_WIDTH + (h + 1) * dh]
        v = y_ref[rs, 2 * C_WIDTH + h * dh:2 * C_WIDTH + (h + 1) * dh]
        qn[key] = q * lax.rsqrt(jnp.sum(q * q, axis=-1, keepdims=True) + RMS_EPS)
        k = k * lax.rsqrt(jnp.sum(k * k, axis=-1, keepdims=True) + RMS_EPS)
        kn[key] = k
        beta_b = jnp.broadcast_to(beta_all[rs, h:h + 1], (c, dh))
        kbeta[key] = k * beta_b
        vbeta[key] = v * beta_b
        gwide[key] = jnp.broadcast_to(g_all[rs, C_HEADS + h:C_HEADS + h + 1], (c, c))

    gdiff = {key: _dot(incl2_ref[...], _split2_rows(gwide[key] * mf_ref[MF_STRICT])) for key in chains}
    gc = {key: jnp.broadcast_to(gdiff[key][:, 0:1] + gwide[key][0:1, 0:1], (c, dh)) for key in chains}
    kf = {key: kn[key].astype(BF16) for key in chains}
    akk = {key: _dot_nt(kbeta[key].astype(BF16), kf[key]) for key in chains}
    aqk = {key: _dot_nt((qn[key] * scale).astype(BF16), kf[key]) for key in chains}
    nlm, xb, intra = {}, {}, {}
    for key in chains:
        e = jnp.exp(gdiff[key])
        nl = akk[key] * (e * mf_ref[MF_NSTRICT])
        nlm[key] = nl.astype(BF16)
        intra[key] = (aqk[key] * (e * mf_ref[MF_INCL])).astype(BF16)
        xb[key] = (nl * mf_ref[MF_PAIR] + mf_ref[MF_EYE]).astype(BF16)

    for lvl in range(n_levels):
        zb = {key: (_dot(xb[key], nlm[key] * mb_ref[lvl]) + mf_ref[MF_EYE]).astype(BF16) for key in chains}
        xb = {key: _dot(zb[key], xb[key]).astype(BF16) for key in chains}

    egc = {key: jnp.exp(gc[key]) for key in chains}
    rhs = {key: jnp.concatenate([vbeta[key], kbeta[key] * egc[key]], axis=1) for key in chains}
    sol = {key: rhs[key] + _dot(xb[key] * mb_ref[n_levels], rhs[key].astype(BF16)) for key in chains}
    wb = {key: sol[key][:, dh:].astype(BF16) for key in chains}
    qg = {key: (qn[key] * egc[key] * scale).astype(BF16) for key in chains}
    kdt, etot = {}, {}
    for key in chains:
        total = gc[key][c - 1:c, :]
        kdt[key] = (kn[key] * jnp.exp(total - gc[key])).T.astype(BF16)
        etot[key] = jnp.exp(total)

    gn = gn_ref[...]
    for ck in range(per_step):
        keys = [(ck, h) for h in range(C_HEADS)]
        rs = slice(ck * c, (ck + 1) * c)
        state = {key: s_ref[key[1]] for key in keys}
        sb = {key: state[key].astype(BF16) for key in keys}
        ws = {key: _dot(wb[key], sb[key]) for key in keys}
        qs = {key: _dot(qg[key], sb[key]) for key in keys}
        vnb = {key: (sol[key][:, :dh] - ws[key]).astype(BF16) for key in keys}
        o = {key: qs[key] + _dot(intra[key], vnb[key]) for key in keys}
        for key in keys:
            s_ref[key[1]] = state[key] * etot[key] + _dot(kdt[key], vnb[key])
        for key in keys:
            sl = slice(key[1] * dh, (key[1] + 1) * dh)
            on = o[key] * lax.rsqrt(jnp.mean(o[key] * o[key], axis=-1, keepdims=True) + RMS_EPS) * gn
            o_ref[rs, sl] = (on * _silu(z_ref[rs, sl])).astype(BF16)


def gdn(cbuf, gates, conv_w, alog_row, dtb_row, norm_g, batch, chunk=GDN_C, per_step=GDN_CHUNKS_PER_STEP):
    t = cbuf.shape[0]
    rows = chunk * per_step
    assert (t // batch) % rows == 0
    ng = t // batch // rows
    hb = rows // V7X_SUBLANES
    w3 = 3 * C_WIDTH
    mf, incl2, mb = _gdn_masks(chunk)
    kern = functools.partial(_gdn_kernel, chunk=chunk, per_step=per_step)
    row = lambda w: pl.BlockSpec((rows, w), lambda b, i: (b * ng + i, 0))
    const = lambda a: pl.BlockSpec(a.shape, lambda b, i: (0,) * a.ndim)
    return pl.pallas_call(
        kern,
        grid=(batch, ng),
        in_specs=[
            row(w3),
            pl.BlockSpec((V7X_SUBLANES, w3), lambda b, i: (jnp.maximum((b * ng + i) * hb - 1, 0), 0)),
            const(conv_w),
            row(V7X_LANES),
            pl.BlockSpec((1, V7X_LANES), lambda b, i: (0, 0)),
            pl.BlockSpec((1, V7X_LANES), lambda b, i: (0, 0)),
            pl.BlockSpec((rows, C_WIDTH), lambda b, i: (b * ng + i, 3)),
            pl.BlockSpec((1, C_HEAD_DIM), lambda b, i: (0, 0)),
            const(mf), const(incl2), const(mb),
        ],
        out_specs=row(C_WIDTH),
        out_shape=jax.ShapeDtypeStruct((t, C_WIDTH), BF16),
        scratch_shapes=[pltpu.VMEM((rows + V7X_SUBLANES, w3), F32),
                        pltpu.VMEM((rows, w3), F32),
                        pltpu.VMEM((C_HEADS, C_HEAD_DIM, C_HEAD_DIM), F32)],
        compiler_params=_cparams(("parallel", "arbitrary")),
        name="gdn",
    )(cbuf, cbuf, conv_w, gates, alog_row, dtb_row, cbuf, norm_g.reshape(1, C_HEAD_DIM), mf, incl2, mb)


def _out_kernel(x_ref, oa_ref, ob_ref, oc_ref, w_ref, g_ref, b_ref, o_ref):
    tm = x_ref.shape[0]
    for r0 in range(0, tm, OUT_LN_ROWS):
        rows = slice(r0, min(r0 + OUT_LN_ROWS, tm))
        mix = _dot(oa_ref[rows, :], w_ref[0:A_WIDTH, :])
        mix = mix + _dot(ob_ref[rows, :], w_ref[A_WIDTH:A_WIDTH + B_WIDTH, :])
        mix = mix + _dot(oc_ref[rows, :], w_ref[A_WIDTH + B_WIDTH:, :])
        y = DEEPNORM_ALPHA * x_ref[rows, :] + mix
        o_ref[rows, :] = _layer_norm(y, g_ref[...], b_ref[...])


def out_ln(x, oa, ob, oc, w_out, g, b, tm=OUT_TM):
    t, d = x.shape
    tm = min(tm, t)
    row = lambda w: pl.BlockSpec((tm, w), lambda i: (i, 0))
    return pl.pallas_call(
        _out_kernel,
        grid=(t // tm,),
        in_specs=[row(d), row(A_WIDTH), row(B_WIDTH), row(C_WIDTH),
                  pl.BlockSpec(w_out.shape, lambda i: (0, 0)),
                  pl.BlockSpec((1, d), lambda i: (0, 0)), pl.BlockSpec((1, d), lambda i: (0, 0))],
        out_specs=row(d),
        out_shape=jax.ShapeDtypeStruct((t, d), F32),
        compiler_params=_cparams(("parallel",)),
        name="out_ln",
    )(x, oa, ob, oc, w_out, g, b)


def _split_w_in(w):
    o = 0
    parts = {}
    for name, n in (("qa", A_WIDTH), ("ka", A_WIDTH), ("va", A_WIDTH), ("qb", B_WIDTH), ("kb", B_WIDTH), ("vb", B_WIDTH),
                    ("qkv", 3 * C_WIDTH), ("z", C_WIDTH), ("beta", C_HEADS), ("a", C_HEADS)):
        parts[name] = w[:, o:o + n]
        o += n
    wk = jnp.concatenate([parts["ka"], parts["kb"]], axis=1).astype(BF16)
    wt = jnp.concatenate([parts["qa"], parts["va"], parts["qb"], parts["vb"]], axis=1).T.astype(BF16)
    wc = jnp.concatenate([parts["qkv"], parts["z"]], axis=1).astype(BF16)
    pad = jnp.zeros((w.shape[0], V7X_LANES - 2 * C_HEADS), w.dtype)
    wg = jnp.concatenate([parts["beta"], parts["a"], pad], axis=1).astype(BF16)
    return wk, wt, wc, wg


def _gate_row(vals):
    row = jnp.zeros((1, V7X_LANES), F32)
    return row.at[0, C_HEADS:2 * C_HEADS].set(vals.astype(F32))


def kernel(x, ffn1_w_gu, ffn1_w_down, ffn2_w_gu, ffn2_w_down, ln_g, ln_b, w_in, conv_w, dn_a_log, dn_dt_bias,
           dn_norm_g, diff_lambda, diff_norm_g, sb_norm_g, w_out):
    batch, seq, d = x.shape
    h = x.reshape(batch * seq, d)
    for l in range(DEPTH):
        lambda_init = 0.8 - 0.6 * math.exp(-0.3 * l)
        ln = lambda i: (ln_g[l, i].reshape(1, d), ln_b[l, i].reshape(1, d))
        h = ffn_ln(h, ffn1_w_gu, ffn1_w_down, l, *ln(0))
        wk, wt, wc, wg = _split_w_in(w_in[l])
        ka, kb, qta, vta, qtb, vtb, cbuf, gates = in_proj(h, wk, wt, wc, wg)
        oa = diff_attn(qta, ka, vta, diff_lambda[l], diff_norm_g[l], batch, lambda_init)
        ob = stick_attn(qtb, kb, vtb, sb_norm_g[l], batch)
        oc = gdn(cbuf, gates, conv_w[l], _gate_row(dn_a_log[l]), _gate_row(dn_dt_bias[l]), dn_norm_g[l], batch)
        h = out_ln(h, oa, ob, oc, w_out[l].astype(BF16), *ln(1))
        h = ffn_ln(h, ffn2_w_gu, ffn2_w_down, l, *ln(2))
    return h.reshape(batch, seq, d)
```
